```python
import jax, jax.numpy as jnp
from jax import lax
import numpy as np

D_MODEL = 1024
BATCH = 8
SEQ = 2048
DEPTH = 1
DEC_BATCH = 128
DEC_SEQ = 1
PAST_LEN = 16384
PAGE_SIZE = 128

N_META = 16
CONV_WIDTH = 31
D_CONV = D_MODEL
N_RET_HEADS = 4
RET_DK = D_MODEL // N_RET_HEADS
RET_DV = 2 * RET_DK
D_QK = N_RET_HEADS * RET_DK
D_V = N_RET_HEADS * RET_DV
CHUNK = 128
ROPE_BASE = 10000.0
N_GROUPS = 4
EXPERTS_PER_GROUP = 4
N_EXPERTS = N_GROUPS * EXPERTS_PER_GROUP
TOP_K_INNER = 2
D_EXPERT = D_MODEL // 2
EPS = 1e-6
D_IN = 2 * D_CONV + 2 * D_QK + 2 * D_V + 2 * D_MODEL

kernel_name = 'hybrid_conv_retention_hmoe_step'


def _rmsnorm(x, g):
    xf = x.astype(jnp.float32)
    y = xf * lax.rsqrt(jnp.mean(xf * xf, -1, keepdims=True) + EPS)
    return (y * g.astype(jnp.float32)).astype(x.dtype)


def _rotary(x, pos):
    half = x.shape[-1] // 2
    freqs = jnp.power(ROPE_BASE, -jnp.arange(half, dtype=jnp.float32) / half)
    ang = pos.astype(jnp.float32)[:, None] * freqs[None, :]
    cos = jnp.cos(ang)[None, :, None, :]
    sin = jnp.sin(ang)[None, :, None, :]
    xf = x.astype(jnp.float32)
    x1, x2 = xf[..., :half], xf[..., half:]
    return jnp.concatenate([x1 * cos - x2 * sin, x1 * sin + x2 * cos], -1)


def _ret_log_gamma():
    return jnp.log(1.0 - jnp.exp2(-5.0 - jnp.arange(N_RET_HEADS, dtype=jnp.float32)))


def _retention_chunk(q, k, v, S, lg):
    L = q.shape[1]
    idx = jnp.arange(L, dtype=jnp.float32)
    diff = idx[:, None] - idx[None, :]
    dmat = jnp.where((diff >= 0)[None], jnp.exp(jnp.maximum(diff, 0.0)[None] * lg[:, None, None]), 0.0)
    scores = jnp.einsum('bihd,bjhd->bhij', q, k) * dmat[None]
    o = jnp.einsum('bhij,bjhe->bihe', scores, v)
    q_dec = q * jnp.exp((idx[:, None] + 1.0) * lg[None, :])[None, :, :, None]
    o = o + jnp.einsum('bihd,bhde->bihe', q_dec, S)
    k_dec = k * jnp.exp((L - 1.0 - idx)[:, None] * lg[None, :])[None, :, :, None]
    S_new = jnp.exp(L * lg)[None, :, None, None] * S + jnp.einsum('bjhd,bjhe->bhde', k_dec, v)
    return o, S_new


def _retention(q, k, v, S0, n_lead):
    lg = _ret_log_gamma()
    outs = []
    S = S0
    if n_lead > 0:
        o, S = _retention_chunk(q[:, :n_lead], k[:, :n_lead], v[:, :n_lead], S, lg)
        outs.append(o)
        q, k, v = q[:, n_lead:], k[:, n_lead:], v[:, n_lead:]
    B, T = q.shape[0], q.shape[1]
    n_full = T // CHUNK
    if n_full > 0:
        Tf = n_full * CHUNK

        def to_chunks(a):
            return jnp.swapaxes(a[:, :Tf].reshape(B, n_full, CHUNK, *a.shape[2:]), 0, 1)

        def step(S, qkv):
            o, S = _retention_chunk(qkv[0], qkv[1], qkv[2], S, lg)
            return S, o

        S, o = lax.scan(step, S, (to_chunks(q), to_chunks(k), to_chunks(v)))
        outs.append(jnp.swapaxes(o, 0, 1).reshape(B, Tf, N_RET_HEADS, RET_DV))
        q, k, v = q[:, Tf:], k[:, Tf:], v[:, Tf:]
    if q.shape[1] > 0:
        o, S = _retention_chunk(q, k, v, S, lg)
        outs.append(o)
    return jnp.concatenate(outs, 1), S


def _mixer(h, S0, buf0, pos, n_lead, w_in, b_gates, conv_w, conv_b, conv_ln_g, conv_ln_b,
           w_conv_out, ret_gn_g, w_ret_out, w_o):
    B, T, _ = h.shape
    proj = h @ w_in
    sizes = [D_CONV, D_CONV, D_QK, D_QK, D_V, D_V, D_MODEL, D_MODEL]
    offs, acc = [], 0
    for s in sizes[:-1]:
        acc += s
        offs.append(acc)
    glu_a, glu_b, q, k, v, g_ret, gate_a, gate_b = jnp.split(proj, offs, axis=-1)

    u = glu_a * jax.nn.sigmoid(glu_b)
    ext = jnp.concatenate([buf0.astype(u.dtype), u], axis=1)
    buf_new = ext[:, ext.shape[1] - (CONV_WIDTH - 1):]
    c = lax.conv_general_dilated(ext, conv_w[:, None, :].astype(ext.dtype), window_strides=(1,),
                                 padding='VALID', dimension_numbers=('NWC', 'WIO', 'NWC'),
                                 feature_group_count=D_CONV) + conv_b
    cf = c.astype(jnp.float32)
    mu = jnp.mean(cf, -1, keepdims=True)
    var = jnp.mean(jnp.square(cf - mu), -1, keepdims=True)
    cn = (cf - mu) * lax.rsqrt(var + EPS) * conv_ln_g.astype(jnp.float32) + conv_ln_b.astype(jnp.float32)
    y_a = jax.nn.silu(cn).astype(h.dtype) @ w_conv_out

    qh = _rotary(q.reshape(B, T, N_RET_HEADS, RET_DK), pos)
    kh = _rotary(k.reshape(B, T, N_RET_HEADS, RET_DK), pos) * (RET_DK ** -0.5)
    vh = v.reshape(B, T, N_RET_HEADS, RET_DV).astype(jnp.float32)
    o, S_new = _retention(qh, kh, vh, S0.astype(jnp.float32), n_lead)
    omu = jnp.mean(o, -1, keepdims=True)
    ovar = jnp.mean(jnp.square(o - omu), -1, keepdims=True)
    on = ((o - omu) * lax.rsqrt(ovar + EPS)).reshape(B, T, D_V) * ret_gn_g.astype(jnp.float32)
    y_b = (on * jax.nn.silu(g_ret.astype(jnp.float32))).astype(h.dtype) @ w_ret_out

    g_a = jax.nn.sigmoid(gate_a + b_gates[:D_MODEL])
    g_b = jax.nn.sigmoid(gate_b + b_gates[D_MODEL:])
    out = (g_a * y_a + g_b * y_b) @ w_o
    return out.astype(h.dtype), S_new.astype(S0.dtype), buf_new.astype(buf0.dtype)


def _hier_moe(h, w_coarse, b_coarse, w_fine, b_fine, w_gate_e, w_up_e, w_down_e):
    B, T, D = h.shape
    t = h.reshape(B * T, D)
    cprob = jax.nn.softmax((t @ w_coarse + b_coarse).astype(jnp.float32), axis=-1)
    g_sel = jnp.argmax(cprob, axis=-1)
    p_g = jnp.take_along_axis(cprob, g_sel[:, None], axis=1)[:, 0]
    fl = (t @ w_fine + b_fine).astype(jnp.float32).reshape(B * T, N_GROUPS, EXPERTS_PER_GROUP)
    sel = jnp.take_along_axis(fl, g_sel[:, None, None], axis=1)[:, 0]
    vals, idx = lax.top_k(sel, TOP_K_INNER)
    w = jax.nn.softmax(vals, axis=-1)
    eid = g_sel[:, None] * EXPERTS_PER_GROUP + idx
    comb = p_g[:, None] * jnp.sum(w[..., None] * jax.nn.one_hot(eid, N_EXPERTS, dtype=jnp.float32), axis=1)
    hg = jnp.einsum('nd,edf->nef', t, w_gate_e)
    hu = jnp.einsum('nd,edf->nef', t, w_up_e)
    act = jax.nn.silu(hg) * hu * comb[:, :, None].astype(hg.dtype)
    out = jnp.einsum('nef,efd->nd', act, w_down_e)
    return out.reshape(B, T, D).astype(h.dtype)


def _layer(x, S0, buf0, pos, n_lead, norm_mix_g, w_in, b_gates, conv_w, conv_b, conv_ln_g, conv_ln_b,
           w_conv_out, ret_gn_g, w_ret_out, w_o, norm_ffn_g, w_coarse, b_coarse, w_fine, b_fine,
           w_gate_e, w_up_e, w_down_e):
    h = _rmsnorm(x, norm_mix_g)
    mix, S_new, buf_new = _mixer(h, S0, buf0, pos, n_lead, w_in, b_gates, conv_w, conv_b, conv_ln_g,
                                 conv_ln_b, w_conv_out, ret_gn_g, w_ret_out, w_o)
    x = x + mix
    x = x + _hier_moe(_rmsnorm(x, norm_ffn_g), w_coarse, b_coarse, w_fine, b_fine, w_gate_e, w_up_e, w_down_e)
    return x, S_new, buf_new


def setup_inputs(seed: int = 0) -> dict:
    key = jax.random.key(seed)
    ks = jax.random.split(key, 32)

    def nrm(k, shape, scale):
        return jax.random.normal(k, shape, jnp.float32) * scale

    L = DEPTH
    return {
        'x_prompt': nrm(ks[0], (BATCH, SEQ, D_MODEL), 1.0),
        'x_sample': nrm(ks[1], (DEC_BATCH, DEC_SEQ, D_MODEL), 1.0),
        'state_ret': nrm(ks[2], (L, DEC_BATCH, N_RET_HEADS, RET_DK, RET_DV), 0.5),
        'state_conv': nrm(ks[3], (L, DEC_BATCH, CONV_WIDTH - 1, D_CONV), 0.5),
        'meta_tokens': nrm(ks[4], (N_META, D_MODEL), 1.0),
        'norm_mix_g': 1.0 + nrm(ks[5], (L, D_MODEL), 0.02),
        'w_in': nrm(ks[6], (L, D_MODEL, D_IN), D_MODEL ** -0.5),
        'b_gates': nrm(ks[7], (L, 2 * D_MODEL), 0.02),
        'conv_w': nrm(ks[8], (L, CONV_WIDTH, D_CONV), CONV_WIDTH ** -0.5),
        'conv_b': nrm(ks[9], (L, D_CONV), 0.02),
        'conv_ln_g': 1.0 + nrm(ks[10], (L, D_CONV), 0.02),
        'conv_ln_b': nrm(ks[11], (L, D_CONV), 0.02),
        'w_conv_out': nrm(ks[12], (L, D_CONV, D_MODEL), D_CONV ** -0.5),
        'ret_gn_g': 1.0 + nrm(ks[13], (L, D_V), 0.02),
        'w_ret_out': nrm(ks[14], (L, D_V, D_MODEL), D_V ** -0.5),
        'w_o': nrm(ks[15], (L, D_MODEL, D_MODEL), D_MODEL ** -0.5),
        'norm_ffn_g': 1.0 + nrm(ks[16], (L, D_MODEL), 0.02),
        'w_coarse': nrm(ks[17], (L, D_MODEL, N_GROUPS), D_MODEL ** -0.5),
        'b_coarse': nrm(ks[18], (L, N_GROUPS), 0.01),
        'w_fine': nrm(ks[19], (L, D_MODEL, N_EXPERTS), D_MODEL ** -0.5),
        'b_fine': nrm(ks[20], (L, N_EXPERTS), 0.01),
        'w_gate_e': nrm(ks[21], (L, N_EXPERTS, D_MODEL, D_EXPERT), D_MODEL ** -0.5),
        'w_up_e': nrm(ks[22], (L, N_EXPERTS, D_MODEL, D_EXPERT), D_MODEL ** -0.5),
        'w_down_e': nrm(ks[23], (L, N_EXPERTS, D_EXPERT, D_MODEL), D_EXPERT ** -0.5),
        'norm_final_g': 1.0 + nrm(ks[24], (D_MODEL,), 0.02),
    }


def reference(x_prompt, x_sample, state_ret, state_conv, meta_tokens, norm_mix_g, w_in, b_gates, conv_w,
              conv_b, conv_ln_g, conv_ln_b, w_conv_out, ret_gn_g, w_ret_out, w_o, norm_ffn_g, w_coarse,
              b_coarse, w_fine, b_fine, w_gate_e, w_up_e, w_down_e, norm_final_g):
    Bp = x_prompt.shape[0]
    meta = jnp.broadcast_to(meta_tokens[None].astype(x_prompt.dtype), (Bp, N_META, D_MODEL))
    yp = jnp.concatenate([meta, x_prompt], axis=1)
    pos_p = jnp.arange(yp.shape[1], dtype=jnp.float32)
    pos_s = PAST_LEN + jnp.arange(x_sample.shape[1], dtype=jnp.float32)
    ys = x_sample
    zero_S = jnp.zeros((Bp, N_RET_HEADS, RET_DK, RET_DV), state_ret.dtype)
    zero_buf = jnp.zeros((Bp, CONV_WIDTH - 1, D_CONV), state_conv.dtype)
    sp, bp, ss, bs = [], [], [], []
    for l in range(DEPTH):
        lp = (norm_mix_g[l], w_in[l], b_gates[l], conv_w[l], conv_b[l], conv_ln_g[l], conv_ln_b[l],
              w_conv_out[l], ret_gn_g[l], w_ret_out[l], w_o[l], norm_ffn_g[l], w_coarse[l], b_coarse[l],
              w_fine[l], b_fine[l], w_gate_e[l], w_up_e[l], w_down_e[l])
        yp, s1, b1 = _layer(yp, zero_S, zero_buf, pos_p, N_META, *lp)
        ys, s2, b2 = _layer(ys, state_ret[l], state_conv[l], pos_s, 0, *lp)
        sp.append(s1)
        bp.append(b1)
        ss.append(s2)
        bs.append(b2)
    y_prompt = _rmsnorm(yp, norm_final_g)[:, N_META:]
    y_sample = _rmsnorm(ys, norm_final_g)
    new_state_ret_prompt = jnp.stack(sp, 0)
    new_state_conv_prompt = jnp.stack(bp, 0)
    new_state_ret_sample = jnp.stack(ss, 0)
    new_state_conv_sample = jnp.stack(bs, 0)
    return (y_prompt, y_sample, new_state_ret_prompt, new_state_conv_prompt, new_state_ret_sample, new_state_conv_sample)
```

```python
import functools

import jax
import jax.numpy as jnp
from jax import lax
from jax.experimental import pallas as pl
from jax.experimental.pallas import tpu as pltpu

D = 1024
N_META = 16
PAST_LEN = 16384
CONV_W = 31
N_HEADS = 4
DK = 256
DV = 512
D_QK = N_HEADS * DK
D_V = N_HEADS * DV
CHUNK = 128
ROPE_BASE = 10000.0
N_GROUPS = 4
EPG = 4
N_EXPERTS = 16
D_EXPERT = 512
EPS = 1e-6
D_IN = 2 * D + 2 * D_QK + 2 * D_V + 2 * D

O_GLU_A, O_GLU_B = 0, D
O_Q, O_K = 2 * D, 2 * D + D_QK
O_V = 2 * D + 2 * D_QK
O_GRET = O_V + D_V
O_GA = O_GRET + D_V
O_GB = O_GA + D

LANES = 128
TILE = CHUNK
PAD_FRONT = TILE - N_META
HALO = 32
VMEM_LIMIT = 56 * 1024 * 1024

BF = jnp.bfloat16
F32 = jnp.float32


def _sigmoid(x):
    return 1.0 / (1.0 + jnp.exp(-x))


def _bdot(a, b):
    return jnp.dot(a.astype(BF), b.astype(BF), preferred_element_type=F32)


def _rmsnorm(x, g):
    return x * lax.rsqrt(jnp.mean(x * x, axis=-1, keepdims=True) + EPS) * g


def _proj_kernel(x_ref, g_ref, w_ref, o_ref):
    h = _rmsnorm(x_ref[...], g_ref[...])
    o_ref[...] = jnp.dot(h.astype(BF), w_ref[...], preferred_element_type=F32)


def _proj_call(x2d, g, w_bf, tm, tn=2048):
    n = x2d.shape[0]
    return pl.pallas_call(
        _proj_kernel,
        grid=(D_IN // tn, n // tm),
        in_specs=[
            pl.BlockSpec((tm, D), lambda j, i: (i, 0)),
            pl.BlockSpec((1, D), lambda j, i: (0, 0)),
            pl.BlockSpec((D, tn), lambda j, i: (0, j)),
        ],
        out_specs=pl.BlockSpec((tm, tn), lambda j, i: (i, j)),
        out_shape=jax.ShapeDtypeStruct((n, D_IN), F32),
        compiler_params=pltpu.CompilerParams(
            dimension_semantics=("arbitrary", "arbitrary"), vmem_limit_bytes=VMEM_LIMIT),
        name="in_proj",
    )(x2d, g, w_bf)


def _route(t, wr_hi_ref, wr_lo_ref, br_ref):
    t_hi = t.astype(BF)
    t_lo = (t - t_hi.astype(F32)).astype(BF)
    wh = wr_hi_ref[...]
    logits = (jnp.dot(t_hi, wh, preferred_element_type=F32)
              + jnp.dot(t_hi, wr_lo_ref[...], preferred_element_type=F32)
              + jnp.dot(t_lo, wh, preferred_element_type=F32)) + br_ref[...]
    lane = lax.broadcasted_iota(jnp.int32, logits.shape, 1)
    neg = jnp.float32(-jnp.inf)
    is_c = (lane >= N_EXPERTS) & (lane < N_EXPERTS + N_GROUPS)
    lc = jnp.where(is_c, logits, neg)
    cmax = jnp.max(lc, axis=-1, keepdims=True)
    csum = jnp.sum(jnp.where(is_c, jnp.exp(lc - cmax), 0.0), axis=-1, keepdims=True)
    p_g = 1.0 / csum
    g_sel = jnp.min(jnp.where(is_c & (lc == cmax), lane - N_EXPERTS, 1 << 20), axis=-1, keepdims=True)
    in_g = (lane < N_EXPERTS) & ((lane >> 2) == g_sel)
    f1 = jnp.where(in_g, logits, neg)
    v1 = jnp.max(f1, axis=-1, keepdims=True)
    i1 = jnp.min(jnp.where(in_g & (f1 == v1), lane, 1 << 20), axis=-1, keepdims=True)
    rest = in_g & (lane != i1)
    f2 = jnp.where(rest, logits, neg)
    v2 = jnp.max(f2, axis=-1, keepdims=True)
    i2 = jnp.min(jnp.where(rest & (f2 == v2), lane, 1 << 20), axis=-1, keepdims=True)
    e2 = jnp.exp(v2 - v1)
    w1 = 1.0 / (1.0 + e2)
    w2 = e2 * w1
    return p_g * jnp.where(lane == i1, w1, jnp.where(lane == i2, w2, 0.0))


def _post_tokens(c, o, g_ret, gate_a, gate_b, x, p):
    mu = jnp.mean(c, axis=-1, keepdims=True)
    dc = c - mu
    var = jnp.mean(dc * dc, axis=-1, keepdims=True)
    cn = dc * lax.rsqrt(var + EPS) * p["ln_g"][...] + p["ln_b"][...]
    y_a = _bdot(cn * _sigmoid(cn), p["w_conv_out"][...])
    parts = []
    for h in range(N_HEADS):
        oh = o[:, h * DV:(h + 1) * DV]
        omu = jnp.mean(oh, axis=-1, keepdims=True)
        od = oh - omu
        ovar = jnp.mean(od * od, axis=-1, keepdims=True)
        parts.append(od * lax.rsqrt(ovar + EPS))
    on = jnp.concatenate(parts, axis=-1) * p["gn_g"][...]
    y_b = _bdot(on * (g_ret * _sigmoid(g_ret)), p["w_ret_out"][...])
    bg = p["b_gates"][...]
    g_a = _sigmoid(gate_a + bg[:, :D])
    g_b = _sigmoid(gate_b + bg[:, D:])
    x1 = x + _bdot(g_a * y_a + g_b * y_b, p["w_o"][...])
    t = _rmsnorm(x1, p["ffn_g"][...])
    return x1, _route(t, p["wr_hi"], p["wr_lo"], p["br"])


_POST_NAMES = ("ln_g", "ln_b", "w_conv_out", "gn_g", "w_ret_out", "b_gates", "w_o", "ffn_g",
               "wr_hi", "wr_lo", "br")


def _post_specs(nidx):
    z2 = lambda *_: (0, 0)
    shapes = {"ln_g": (1, D), "ln_b": (1, D), "w_conv_out": (D, D), "gn_g": (1, D_V),
              "w_ret_out": (D_V, D), "b_gates": (1, 2 * D), "w_o": (D, D), "ffn_g": (1, D),
              "wr_hi": (D, LANES), "wr_lo": (D, LANES), "br": (1, LANES)}
    return [pl.BlockSpec(shapes[k], z2) for k in _POST_NAMES]


def _rotary(x, cos, sin):
    x1, x2 = x[:, :LANES], x[:, LANES:]
    return jnp.concatenate([x1 * cos - x2 * sin, x1 * sin + x2 * cos], axis=-1)


def _prompt_mix_kernel(x_ref, proj_ref, cos_ref, sin_ref, dmat_ref, dq_ref, dk_ref, gl_ref,
                       cw_ref, cb_ref, *rest):
    post_refs = rest[:len(_POST_NAMES)]
    x1_ref, comb_ref, s_out_ref, buf_out_ref, ext_ref, s_ref, c_ref, o_ref = rest[len(_POST_NAMES):]
    p = dict(zip(_POST_NAMES, post_refs))
    i = pl.program_id(1)

    @pl.when(i == 0)
    def _():
        ext_ref[0:HALO, :] = jnp.zeros((HALO, D), F32)
        s_ref[...] = jnp.zeros_like(s_ref)

    glu_a = proj_ref[0, :, O_GLU_A:O_GLU_A + D]
    glu_b = proj_ref[0, :, O_GLU_B:O_GLU_B + D]
    ext_ref[HALO:HALO + TILE, :] = glu_a * _sigmoid(glu_b)
    base = HALO - (CONV_W - 1)
    for cb in range(D // LANES):
        cols = slice(cb * LANES, (cb + 1) * LANES)
        acc = jnp.broadcast_to(cb_ref[:, cols], (TILE, LANES))
        for j in range(CONV_W):
            acc = acc + cw_ref[j:j + 1, cols] * ext_ref[base + j:base + j + TILE, cols]
        c_ref[:, cols] = acc
    ext_ref[0:HALO, :] = ext_ref[TILE:TILE + HALO, :]

    cos = cos_ref[...]
    sin = sin_ref[...]
    for h in range(N_HEADS):
        q = _rotary(proj_ref[0, :, O_Q + h * DK:O_Q + (h + 1) * DK], cos, sin)
        k = _rotary(proj_ref[0, :, O_K + h * DK:O_K + (h + 1) * DK], cos, sin) * (DK ** -0.5)
        v = proj_ref[0, :, O_V + h * DV:O_V + (h + 1) * DV].astype(BF)
        dq = dq_ref[h]
        dk = dk_ref[h]
        qb = q.astype(BF)
        scores = lax.dot_general(qb, k.astype(BF), (((1,), (1,)), ((), ())),
                                 preferred_element_type=F32) * dmat_ref[h]
        s_old = s_ref[h]
        q_dec = jnp.concatenate([q[:, :LANES] * dq, q[:, LANES:] * dq], axis=-1)
        k_dec = jnp.concatenate([k[:, :LANES] * dk, k[:, LANES:] * dk], axis=-1)
        o_ref[:, h * DV:(h + 1) * DV] = (jnp.dot(scores.astype(BF), v, preferred_element_type=F32)
                                         + _bdot(q_dec, s_old))
        s_ref[h] = gl_ref[h] * s_old + lax.dot_general(
            k_dec.astype(BF), v, (((0,), (0,)), ((), ())), preferred_element_type=F32)

    @pl.when(i == pl.num_programs(1) - 1)
    def _():
        s_out_ref[0] = s_ref[...]
        buf_out_ref[0] = ext_ref[0:HALO, :]

    x1, comb = _post_tokens(c_ref[...], o_ref[...],
                            proj_ref[0, :, O_GRET:O_GRET + D_V],
                            proj_ref[0, :, O_GA:O_GA + D],
                            proj_ref[0, :, O_GB:O_GB + D],
                            x_ref[0], p)
    x1_ref[0] = x1
    comb_ref[0] = comb


def _prompt_mix(xpad, proj, tabs, cw, cb, post_args):
    b, tp, _ = xpad.shape
    nt = tp // TILE
    seq = tp - TILE
    cos, sin, dmat, dq, dk, gl = tabs
    out_tok = lambda bi, i: (bi, jnp.maximum(i - 1, 0), 0)
    z2 = lambda bi, i: (0, 0)
    z3 = lambda bi, i: (0, 0, 0)
    return pl.pallas_call(
        _prompt_mix_kernel,
        grid=(b, nt),
        in_specs=[
            pl.BlockSpec((1, TILE, D), lambda bi, i: (bi, i, 0)),
            pl.BlockSpec((1, TILE, D_IN), lambda bi, i: (bi, i, 0)),
            pl.BlockSpec((TILE, LANES), lambda bi, i: (i, 0)),
            pl.BlockSpec((TILE, LANES), lambda bi, i: (i, 0)),
            pl.BlockSpec((N_HEADS, TILE, TILE), z3),
            pl.BlockSpec((N_HEADS, TILE, LANES), z3),
            pl.BlockSpec((N_HEADS, TILE, LANES), z3),
            pl.BlockSpec(memory_space=pltpu.SMEM),
            pl.BlockSpec((CONV_W, D), z2),
            pl.BlockSpec((1, D), z2),
        ] + _post_specs(2),
        out_specs=[
            pl.BlockSpec((1, TILE, D), out_tok),
            pl.BlockSpec((1, TILE, LANES), out_tok),
            pl.BlockSpec((1, N_HEADS, DK, DV), lambda bi, i: (bi, 0, 0, 0)),
            pl.BlockSpec((1, HALO, D), lambda bi, i: (bi, 0, 0)),
        ],
        out_shape=[
            jax.ShapeDtypeStruct((b, seq, D), F32),
            jax.ShapeDtypeStruct((b, seq, LANES), F32),
            jax.ShapeDtypeStruct((b, N_HEADS, DK, DV), F32),
            jax.ShapeDtypeStruct((b, HALO, D), F32),
        ],
        scratch_shapes=[
            pltpu.VMEM((HALO + TILE, D), F32),
            pltpu.VMEM((N_HEADS, DK, DV), F32),
            pltpu.VMEM((TILE, D), F32),
            pltpu.VMEM((TILE, D_V), F32),
        ],
        compiler_params=pltpu.CompilerParams(
            dimension_semantics=("arbitrary", "arbitrary"), vmem_limit_bytes=VMEM_LIMIT),
        name="prompt_mix",
    )(xpad, proj, cos, sin, dmat, dq, dk, gl, cw, cb, *post_args)


SEQ_BLK = 2
MXU_ROWS = 16


def _sample_seq_kernel(proj_ref, s_ref, sc_ref, cos_ref, sin_ref, gam_ref, cw_ref, cb_ref,
                       o_ref, c_ref, s_out_ref, sc_out_ref):
    cos = cos_ref[...]
    sin = sin_ref[...]
    u = proj_ref[0, :, O_GLU_A:O_GLU_A + D] * _sigmoid(proj_ref[0, :, O_GLU_B:O_GLU_B + D])
    row = lax.broadcasted_iota(jnp.int32, (MXU_ROWS, DK), 0)
    for s in range(SEQ_BLK):
        us = u[s:s + 1, :]
        c_ref[0, s:s + 1, :] = (
            jnp.sum(cw_ref[0:CONV_W - 1, :] * sc_ref[s], axis=0, keepdims=True)
            + cw_ref[CONV_W - 1:CONV_W, :] * us + cb_ref[...])
        sc_out_ref[s, 0:CONV_W - 2, :] = sc_ref[s, 1:CONV_W - 1, :]
        sc_out_ref[s, CONV_W - 2:CONV_W - 1, :] = us
        for h in range(N_HEADS):
            q = _rotary(proj_ref[0, s:s + 1, O_Q + h * DK:O_Q + (h + 1) * DK], cos, sin)
            k = _rotary(proj_ref[0, s:s + 1, O_K + h * DK:O_K + (h + 1) * DK], cos, sin) * (DK ** -0.5)
            v = proj_ref[0, s:s + 1, O_V + h * DV:O_V + (h + 1) * DV]
            gam = gam_ref[h]
            qk = jnp.sum(q * k, axis=-1, keepdims=True)
            s_old = s_ref[s, h]
            o8 = _bdot(jnp.broadcast_to(q * gam, (MXU_ROWS, DK)), s_old)
            o_ref[0, s:s + 1, h * DV:(h + 1) * DV] = qk * v + o8[0:1, :]
            k8 = jnp.where(row == 0, jnp.broadcast_to(k, (MXU_ROWS, DK)), 0.0).astype(BF)
            v8 = jnp.broadcast_to(v, (MXU_ROWS, DV)).astype(BF)
            kv = lax.dot_general(k8, v8, (((0,), (0,)), ((), ())), preferred_element_type=F32)
            s_out_ref[s, h] = gam * s_old + kv


def _sample_seq(proj_s, state_ret, state_conv, cos_s, sin_s, gam, cw, cb):
    n = proj_s.shape[0]
    nb = n // SEQ_BLK
    proj3 = proj_s.reshape(nb, SEQ_BLK, D_IN)
    z2 = lambda i: (0, 0)
    o, c, s_new, sc_new = pl.pallas_call(
        _sample_seq_kernel,
        grid=(nb,),
        in_specs=[
            pl.BlockSpec((1, SEQ_BLK, D_IN), lambda i: (i, 0, 0)),
            pl.BlockSpec((SEQ_BLK, N_HEADS, DK, DV), lambda i: (i, 0, 0, 0)),
            pl.BlockSpec((SEQ_BLK, CONV_W - 1, D), lambda i: (i, 0, 0)),
            pl.BlockSpec((1, LANES), z2),
            pl.BlockSpec((1, LANES), z2),
            pl.BlockSpec(memory_space=pltpu.SMEM),
            pl.BlockSpec((CONV_W, D), z2),
            pl.BlockSpec((1, D), z2),
        ],
        out_specs=[
            pl.BlockSpec((1, SEQ_BLK, D_V), lambda i: (i, 0, 0)),
            pl.BlockSpec((1, SEQ_BLK, D), lambda i: (i, 0, 0)),
            pl.BlockSpec((SEQ_BLK, N_HEADS, DK, DV), lambda i: (i, 0, 0, 0)),
            pl.BlockSpec((SEQ_BLK, CONV_W - 1, D), lambda i: (i, 0, 0)),
        ],
        out_shape=[
            jax.ShapeDtypeStruct((nb, SEQ_BLK, D_V), F32),
            jax.ShapeDtypeStruct((nb, SEQ_BLK, D), F32),
            jax.ShapeDtypeStruct(state_ret.shape, F32),
            jax.ShapeDtypeStruct(state_conv.shape, F32),
        ],
        compiler_params=pltpu.CompilerParams(
            dimension_semantics=("arbitrary",), vmem_limit_bytes=VMEM_LIMIT),
        name="sample_seq",
    )(proj3, state_ret, state_conv, cos_s, sin_s, gam, cw, cb)
    return o.reshape(n, D_V), c.reshape(n, D), s_new, sc_new


def _post_kernel(c_ref, o_ref, gret_ref, gates_ref, x_ref, *rest):
    p = dict(zip(_POST_NAMES, rest[:len(_POST_NAMES)]))
    x1_ref, comb_ref = rest[len(_POST_NAMES):]
    x1, comb = _post_tokens(c_ref[...], o_ref[...], gret_ref[...], gates_ref[:, :D], gates_ref[:, D:],
                            x_ref[...], p)
    x1_ref[...] = x1
    comb_ref[...] = comb


def _post_call(c, o, proj_s, x_s, post_args):
    n = x_s.shape[0]
    z2 = lambda i: (0, 0)
    return pl.pallas_call(
        _post_kernel,
        grid=(1,),
        in_specs=[
            pl.BlockSpec((n, D), z2),
            pl.BlockSpec((n, D_V), z2),
            pl.BlockSpec((n, D_V), lambda i: (0, O_GRET // D_V)),
            pl.BlockSpec((n, 2 * D), lambda i: (0, O_GA // (2 * D))),
            pl.BlockSpec((n, D), z2),
        ] + _post_specs(1),
        out_specs=[pl.BlockSpec((n, D), z2), pl.BlockSpec((n, LANES), z2)],
        out_shape=[jax.ShapeDtypeStruct((n, D), F32), jax.ShapeDtypeStruct((n, LANES), F32)],
        compiler_params=pltpu.CompilerParams(
            dimension_semantics=("arbitrary",), vmem_limit_bytes=VMEM_LIMIT),
        name="sample_post",
    )(c, o, proj_s, proj_s, x_s, *post_args)


def _moe_kernel(x1_ref, comb_ref, fg_ref, ng_ref, wg_ref, wu_ref, wd_ref, y_ref, t_ref, acc_ref):
    e = pl.program_id(1)

    @pl.when(e == 0)
    def _():
        t_ref[...] = _rmsnorm(x1_ref[...], fg_ref[...]).astype(BF)
        acc_ref[...] = jnp.zeros_like(acc_ref)

    t = t_ref[...]
    hg = jnp.dot(t, wg_ref[0], preferred_element_type=F32)
    hu = jnp.dot(t, wu_ref[0], preferred_element_type=F32)
    comb = comb_ref[...]
    lane = lax.broadcasted_iota(jnp.int32, comb.shape, 1)
    ce = jnp.sum(jnp.where(lane == e, comb, 0.0), axis=-1, keepdims=True)
    act = (hg * _sigmoid(hg)) * hu * ce
    acc_ref[...] += jnp.dot(act.astype(BF), wd_ref[0], preferred_element_type=F32)

    @pl.when(e == N_EXPERTS - 1)
    def _():
        y_ref[...] = _rmsnorm(x1_ref[...] + acc_ref[...], ng_ref[...])


def _moe_call(x1, comb, ffn_g, final_g, wg, wu, wd, tm):
    n = x1.shape[0]
    z2 = lambda i, e: (0, 0)
    return pl.pallas_call(
        _moe_kernel,
        grid=(n // tm, N_EXPERTS),
        in_specs=[
            pl.BlockSpec((tm, D), lambda i, e: (i, 0)),
            pl.BlockSpec((tm, LANES), lambda i, e: (i, 0)),
            pl.BlockSpec((1, D), z2),
            pl.BlockSpec((1, D), z2),
            pl.BlockSpec((1, D, D_EXPERT), lambda i, e: (e, 0, 0)),
            pl.BlockSpec((1, D, D_EXPERT), lambda i, e: (e, 0, 0)),
            pl.BlockSpec((1, D_EXPERT, D), lambda i, e: (e, 0, 0)),
        ],
        out_specs=pl.BlockSpec((tm, D), lambda i, e: (i, 0)),
        out_shape=jax.ShapeDtypeStruct((n, D), F32),
        scratch_shapes=[pltpu.VMEM((tm, D), BF), pltpu.VMEM((tm, D), F32)],
        compiler_params=pltpu.CompilerParams(
            dimension_semantics=("arbitrary", "arbitrary"), vmem_limit_bytes=VMEM_LIMIT),
        name="moe",
    )(x1, comb, ffn_g, final_g, wg, wu, wd)


def _rope_tables(pos):
    half = DK // 2
    freqs = jnp.power(ROPE_BASE, -jnp.arange(half, dtype=F32) / half)
    ang = pos.astype(F32)[:, None] * freqs[None, :]
    return jnp.cos(ang), jnp.sin(ang)


def _decay_tables(L):
    lg = jnp.log(1.0 - jnp.exp2(-5.0 - jnp.arange(N_HEADS, dtype=F32)))
    idx = jnp.arange(L, dtype=F32)
    diff = idx[:, None] - idx[None, :]
    dmat = jnp.where((diff >= 0)[None], jnp.exp(jnp.maximum(diff, 0.0)[None] * lg[:, None, None]), 0.0)
    dq = jnp.exp((idx[:, None] + 1.0) * lg[None, :]).T
    dk = jnp.exp((L - 1.0 - idx)[:, None] * lg[None, :]).T
    gl = jnp.exp(L * lg)
    bl = lambda a: jnp.broadcast_to(a[:, :, None], (N_HEADS, L, LANES))
    return dmat, bl(dq), bl(dk), gl


def kernel(x_prompt, x_sample, state_ret, state_conv, meta_tokens, norm_mix_g, w_in, b_gates, conv_w, conv_b, conv_ln_g, conv_ln_b, w_conv_out, ret_gn_g, w_ret_out, w_o, norm_ffn_g, w_coarse, b_coarse, w_fine, b_fine, w_gate_e, w_up_e, w_down_e, norm_final_g):
    bp, seq, _ = x_prompt.shape
    ns = x_sample.shape[0]
    l = 0
    row = lambda a: a.reshape(1, -1)

    w_in_bf = w_in[l].astype(BF)
    wr = jnp.zeros((D, LANES), F32).at[:, :N_EXPERTS].set(w_fine[l]).at[:, N_EXPERTS:N_EXPERTS + N_GROUPS].set(w_coarse[l])
    wr_hi = wr.astype(BF)
    wr_lo = (wr - wr_hi.astype(F32)).astype(BF)
    br = jnp.zeros((1, LANES), F32).at[0, :N_EXPERTS].set(b_fine[l]).at[0, N_EXPERTS:N_EXPERTS + N_GROUPS].set(b_coarse[l])
    post_args = (row(conv_ln_g[l]), row(conv_ln_b[l]), w_conv_out[l].astype(BF), row(ret_gn_g[l]),
                 w_ret_out[l].astype(BF), row(b_gates[l]), w_o[l].astype(BF), row(norm_ffn_g[l]),
                 wr_hi, wr_lo, br)
    wg = w_gate_e[l].astype(BF)
    wu = w_up_e[l].astype(BF)
    wd = w_down_e[l].astype(BF)
    mix_g = row(norm_mix_g[l])
    cw = conv_w[l]
    cb = row(conv_b[l])

    tp = PAD_FRONT + N_META + seq
    xpad = jnp.concatenate([
        jnp.zeros((bp, PAD_FRONT, D), F32),
        jnp.broadcast_to(meta_tokens[None], (bp, N_META, D)),
        x_prompt], axis=1)
    proj_p = _proj_call(xpad.reshape(bp * tp, D), mix_g, w_in_bf, tm=256).reshape(bp, tp, D_IN)
    pos_p = jnp.maximum(jnp.arange(tp, dtype=F32) - PAD_FRONT, 0.0)
    cos_p, sin_p = _rope_tables(pos_p)
    dmat, dq, dk, gl = _decay_tables(TILE)
    x1_p, comb_p, s_new_p, buf_p = _prompt_mix(xpad, proj_p, (cos_p, sin_p, dmat, dq, dk, gl), cw, cb, post_args)
    y_p = _moe_call(x1_p.reshape(bp * seq, D), comb_p.reshape(bp * seq, LANES), row(norm_ffn_g[l]),
                    row(norm_final_g), wg, wu, wd, tm=512).reshape(bp, seq, D)

    xs = x_sample.reshape(ns, D)
    proj_s = _proj_call(xs, mix_g, w_in_bf, tm=ns)
    pos_s = jnp.full((1,), float(PAST_LEN), F32)
    cos_s, sin_s = _rope_tables(pos_s)
    _, _, _, gam = _decay_tables(1)
    o_s, c_s, s_new_s, buf_s = _sample_seq(proj_s, state_ret[l], state_conv[l], cos_s, sin_s, gam, cw, cb)
    x1_s, comb_s = _post_call(c_s, o_s, proj_s, xs, post_args)
    y_s = _moe_call(x1_s, comb_s, row(norm_ffn_g[l]), row(norm_final_g), wg, wu, wd, tm=ns).reshape(ns, 1, D)

    return (y_p, y_s, s_new_p[None], buf_p[:, HALO - (CONV_W - 1):][None], s_new_s[None], buf_s[None])
```

```python
import functools

import jax
import jax.numpy as jnp
from jax import lax
from jax.experimental import pallas as pl
from jax.experimental.pallas import tpu as pltpu

D = 1024
N_META = 16
PAST_LEN = 16384
CONV_W = 31
N_HEADS = 4
DK = 256
DV = 512
D_QK = N_HEADS * DK
D_V = N_HEADS * DV
CHUNK = 128
ROPE_BASE = 10000.0
N_GROUPS = 4
EPG = 4
N_EXPERTS = 16
D_EXPERT = 512
EPS = 1e-6
D_IN = 2 * D + 2 * D_QK + 2 * D_V + 2 * D

O_GLU_A, O_GLU_B = 0, D
O_Q, O_K = 2 * D, 2 * D + D_QK
O_V = 2 * D + 2 * D_QK
O_GRET = O_V + D_V
O_GA = O_GRET + D_V
O_GB = O_GA + D

N_PAIRS = EPG * (EPG - 1) // 2
N_BINS = N_GROUPS * N_PAIRS

LANES = 128
ROW_W = D + LANES
AUX_W_LO, AUX_W_HI, AUX_BIN, AUX_RANK = 0, 1, 2, 3
MOE_TM = 256
DMA_UNROLL = 8
TILE = CHUNK
PAD_FRONT = TILE - N_META
HALO = 32
VMEM_LIMIT = 56 * 1024 * 1024

BF = jnp.bfloat16
F32 = jnp.float32


def _sigmoid(x):
    return 1.0 / (1.0 + jnp.exp(-x))


def _bdot(a, b):
    return jnp.dot(a.astype(BF), b.astype(BF), preferred_element_type=F32)


def _rmsnorm(x, g):
    return x * lax.rsqrt(jnp.mean(x * x, axis=-1, keepdims=True) + EPS) * g


def _proj_kernel(x_ref, g_ref, w_ref, o_ref):
    h = _rmsnorm(x_ref[...], g_ref[...])
    o_ref[...] = jnp.dot(h.astype(BF), w_ref[...], preferred_element_type=F32)


def _proj_call(x2d, g, w_bf, tm, tn=2048):
    n = x2d.shape[0]
    return pl.pallas_call(
        _proj_kernel,
        grid=(D_IN // tn, n // tm),
        in_specs=[
            pl.BlockSpec((tm, D), lambda j, i: (i, 0)),
            pl.BlockSpec((1, D), lambda j, i: (0, 0)),
            pl.BlockSpec((D, tn), lambda j, i: (0, j)),
        ],
        out_specs=pl.BlockSpec((tm, tn), lambda j, i: (i, j)),
        out_shape=jax.ShapeDtypeStruct((n, D_IN), F32),
        compiler_params=pltpu.CompilerParams(
            dimension_semantics=("arbitrary", "arbitrary"), vmem_limit_bytes=VMEM_LIMIT),
        name="in_proj",
    )(x2d, g, w_bf)


def _route(t, wr_hi_ref, wr_lo_ref, br_ref):
    t_hi = t.astype(BF)
    t_lo = (t - t_hi.astype(F32)).astype(BF)
    wh = wr_hi_ref[...]
    logits = (jnp.dot(t_hi, wh, preferred_element_type=F32)
              + jnp.dot(t_hi, wr_lo_ref[...], preferred_element_type=F32)
              + jnp.dot(t_lo, wh, preferred_element_type=F32)) + br_ref[...]
    lane = lax.broadcasted_iota(jnp.int32, logits.shape, 1)
    neg = jnp.float32(-jnp.inf)
    is_c = (lane >= N_EXPERTS) & (lane < N_EXPERTS + N_GROUPS)
    lc = jnp.where(is_c, logits, neg)
    cmax = jnp.max(lc, axis=-1, keepdims=True)
    csum = jnp.sum(jnp.where(is_c, jnp.exp(lc - cmax), 0.0), axis=-1, keepdims=True)
    p_g = 1.0 / csum
    g_sel = jnp.min(jnp.where(is_c & (lc == cmax), lane - N_EXPERTS, 1 << 20), axis=-1, keepdims=True)
    in_g = (lane < N_EXPERTS) & ((lane >> 2) == g_sel)
    f1 = jnp.where(in_g, logits, neg)
    v1 = jnp.max(f1, axis=-1, keepdims=True)
    i1 = jnp.min(jnp.where(in_g & (f1 == v1), lane, 1 << 20), axis=-1, keepdims=True)
    rest = in_g & (lane != i1)
    f2 = jnp.where(rest, logits, neg)
    v2 = jnp.max(f2, axis=-1, keepdims=True)
    i2 = jnp.min(jnp.where(rest & (f2 == v2), lane, 1 << 20), axis=-1, keepdims=True)
    e2 = jnp.exp(v2 - v1)
    w1 = p_g / (1.0 + e2)
    w2 = e2 * w1
    comb = jnp.where(lane == i1, w1, jnp.where(lane == i2, w2, 0.0))
    first_low = i1 < i2
    a = jnp.where(first_low, i1, i2) & (EPG - 1)
    b = jnp.where(first_low, i2, i1) & (EPG - 1)
    pair = jnp.where(a == 0, 0, jnp.where(a == 1, 3, 5)) + (b - a - 1)
    bin_id = g_sel * N_PAIRS + pair
    return comb, (jnp.where(first_low, w1, w2), jnp.where(first_low, w2, w1), bin_id)


def _post_tokens(c, o, g_ret, gate_a, gate_b, x, p):
    mu = jnp.mean(c, axis=-1, keepdims=True)
    dc = c - mu
    var = jnp.mean(dc * dc, axis=-1, keepdims=True)
    cn = dc * lax.rsqrt(var + EPS) * p["ln_g"][...] + p["ln_b"][...]
    y_a = _bdot(cn * _sigmoid(cn), p["w_conv_out"][...])
    parts = []
    for h in range(N_HEADS):
        oh = o[:, h * DV:(h + 1) * DV]
        omu = jnp.mean(oh, axis=-1, keepdims=True)
        od = oh - omu
        ovar = jnp.mean(od * od, axis=-1, keepdims=True)
        parts.append(od * lax.rsqrt(ovar + EPS))
    on = jnp.concatenate(parts, axis=-1) * p["gn_g"][...]
    y_b = _bdot(on * (g_ret * _sigmoid(g_ret)), p["w_ret_out"][...])
    bg = p["b_gates"][...]
    g_a = _sigmoid(gate_a + bg[:, :D])
    g_b = _sigmoid(gate_b + bg[:, D:])
    x1 = x + _bdot(g_a * y_a + g_b * y_b, p["w_o"][...])
    t = _rmsnorm(x1, p["ffn_g"][...])
    comb, pair_route = _route(t, p["wr_hi"], p["wr_lo"], p["br"])
    return x1, comb, pair_route


_POST_NAMES = ("ln_g", "ln_b", "w_conv_out", "gn_g", "w_ret_out", "b_gates", "w_o", "ffn_g",
               "wr_hi", "wr_lo", "br")


def _post_specs(nidx):
    z2 = lambda *_: (0, 0)
    shapes = {"ln_g": (1, D), "ln_b": (1, D), "w_conv_out": (D, D), "gn_g": (1, D_V),
              "w_ret_out": (D_V, D), "b_gates": (1, 2 * D), "w_o": (D, D), "ffn_g": (1, D),
              "wr_hi": (D, LANES), "wr_lo": (D, LANES), "br": (1, LANES)}
    return [pl.BlockSpec(shapes[k], z2) for k in _POST_NAMES]


def _rotary(x, cos, sin):
    x1, x2 = x[:, :LANES], x[:, LANES:]
    return jnp.concatenate([x1 * cos - x2 * sin, x1 * sin + x2 * cos], axis=-1)


def _prompt_mix_kernel(x_ref, proj_ref, cos_ref, sin_ref, dmat_ref, dq_ref, dk_ref, gl_ref,
                       cw_ref, cb_ref, tri_ref, *rest):
    post_refs = rest[:len(_POST_NAMES)]
    (x1e_ref, cnt_out_ref, s_out_ref, buf_out_ref,
     ext_ref, s_ref, c_ref, o_ref, cnt_ref) = rest[len(_POST_NAMES):]
    p = dict(zip(_POST_NAMES, post_refs))
    i = pl.program_id(1)

    @pl.when(i == 0)
    def _():
        ext_ref[0:HALO, :] = jnp.zeros((HALO, D), F32)
        s_ref[...] = jnp.zeros_like(s_ref)

    glu_a = proj_ref[0, :, O_GLU_A:O_GLU_A + D]
    glu_b = proj_ref[0, :, O_GLU_B:O_GLU_B + D]
    ext_ref[HALO:HALO + TILE, :] = glu_a * _sigmoid(glu_b)
    base = HALO - (CONV_W - 1)
    for cb in range(D // LANES):
        cols = slice(cb * LANES, (cb + 1) * LANES)
        acc = jnp.broadcast_to(cb_ref[:, cols], (TILE, LANES))
        for j in range(CONV_W):
            acc = acc + cw_ref[j:j + 1, cols] * ext_ref[base + j:base + j + TILE, cols]
        c_ref[:, cols] = acc
    ext_ref[0:HALO, :] = ext_ref[TILE:TILE + HALO, :]

    cos = cos_ref[...]
    sin = sin_ref[...]
    for h in range(N_HEADS):
        q = _rotary(proj_ref[0, :, O_Q + h * DK:O_Q + (h + 1) * DK], cos, sin)
        k = _rotary(proj_ref[0, :, O_K + h * DK:O_K + (h + 1) * DK], cos, sin) * (DK ** -0.5)
        v = proj_ref[0, :, O_V + h * DV:O_V + (h + 1) * DV].astype(BF)
        dq = dq_ref[h]
        dk = dk_ref[h]
        qb = q.astype(BF)
        scores = lax.dot_general(qb, k.astype(BF), (((1,), (1,)), ((), ())),
                                 preferred_element_type=F32) * dmat_ref[h]
        s_old = s_ref[h]
        q_dec = jnp.concatenate([q[:, :LANES] * dq, q[:, LANES:] * dq], axis=-1)
        k_dec = jnp.concatenate([k[:, :LANES] * dk, k[:, LANES:] * dk], axis=-1)
        o_ref[:, h * DV:(h + 1) * DV] = (jnp.dot(scores.astype(BF), v, preferred_element_type=F32)
                                         + _bdot(q_dec, s_old))
        s_ref[h] = gl_ref[h] * s_old + lax.dot_general(
            k_dec.astype(BF), v, (((0,), (0,)), ((), ())), preferred_element_type=F32)

    @pl.when(i == pl.num_programs(1) - 1)
    def _():
        s_out_ref[0] = s_ref[...]
        buf_out_ref[0] = ext_ref[0:HALO, :]

    x1, _, (w_lo, w_hi, bin_id) = _post_tokens(c_ref[...], o_ref[...],
                                               proj_ref[0, :, O_GRET:O_GRET + D_V],
                                               proj_ref[0, :, O_GA:O_GA + D],
                                               proj_ref[0, :, O_GB:O_GB + D],
                                               x_ref[0], p)

    @pl.when((pl.program_id(0) == 0) & (i == 0))
    def _():
        cnt_ref[...] = jnp.zeros_like(cnt_ref)

    lane = lax.broadcasted_iota(jnp.int32, (TILE, LANES), 1)
    onehot = lane == bin_id
    earlier = jnp.dot(tri_ref[...], onehot.astype(BF), preferred_element_type=F32)
    rank = jnp.sum(jnp.where(onehot, earlier + cnt_ref[...], 0.0), axis=-1, keepdims=True)

    @pl.when(i > 0)
    def _():
        cnt_ref[...] += jnp.sum(onehot.astype(F32), axis=0, keepdims=True)

    x1e_ref[0, :, 0:D] = x1
    x1e_ref[0, :, D:D + LANES] = jnp.where(
        lane == AUX_W_LO, w_lo, jnp.where(lane == AUX_W_HI, w_hi, jnp.where(
            lane == AUX_BIN, bin_id.astype(F32), jnp.where(lane == AUX_RANK, rank, 0.0))))
    cnt_out_ref[...] = cnt_ref[...]


def _prompt_mix(xpad, proj, tabs, cw, cb, post_args):
    b, tp, _ = xpad.shape
    nt = tp // TILE
    seq = tp - TILE
    cos, sin, dmat, dq, dk, gl = tabs
    out_tok = lambda bi, i: (bi, jnp.maximum(i - 1, 0), 0)
    z2 = lambda bi, i: (0, 0)
    z3 = lambda bi, i: (0, 0, 0)
    r = jnp.arange(TILE)
    tri = (r[None, :] < r[:, None]).astype(BF)
    return pl.pallas_call(
        _prompt_mix_kernel,
        grid=(b, nt),
        in_specs=[
            pl.BlockSpec((1, TILE, D), lambda bi, i: (bi, i, 0)),
            pl.BlockSpec((1, TILE, D_IN), lambda bi, i: (bi, i, 0)),
            pl.BlockSpec((TILE, LANES), lambda bi, i: (i, 0)),
            pl.BlockSpec((TILE, LANES), lambda bi, i: (i, 0)),
            pl.BlockSpec((N_HEADS, TILE, TILE), z3),
            pl.BlockSpec((N_HEADS, TILE, LANES), z3),
            pl.BlockSpec((N_HEADS, TILE, LANES), z3),
            pl.BlockSpec(memory_space=pltpu.SMEM),
            pl.BlockSpec((CONV_W, D), z2),
            pl.BlockSpec((1, D), z2),
            pl.BlockSpec((TILE, TILE), z2),
        ] + _post_specs(2),
        out_specs=[
            pl.BlockSpec((1, TILE, ROW_W), out_tok),
            pl.BlockSpec((1, LANES), z2),
            pl.BlockSpec((1, N_HEADS, DK, DV), lambda bi, i: (bi, 0, 0, 0)),
            pl.BlockSpec((1, HALO, D), lambda bi, i: (bi, 0, 0)),
        ],
        out_shape=[
            jax.ShapeDtypeStruct((b, seq, ROW_W), F32),
            jax.ShapeDtypeStruct((1, LANES), F32),
            jax.ShapeDtypeStruct((b, N_HEADS, DK, DV), F32),
            jax.ShapeDtypeStruct((b, HALO, D), F32),
        ],
        scratch_shapes=[
            pltpu.VMEM((HALO + TILE, D), F32),
            pltpu.VMEM((N_HEADS, DK, DV), F32),
            pltpu.VMEM((TILE, D), F32),
            pltpu.VMEM((TILE, D_V), F32),
            pltpu.VMEM((1, LANES), F32),
        ],
        compiler_params=pltpu.CompilerParams(
            dimension_semantics=("arbitrary", "arbitrary"), vmem_limit_bytes=VMEM_LIMIT),
        name="prompt_mix",
    )(xpad, proj, cos, sin, dmat, dq, dk, gl, cw, cb, tri, *post_args)


SEQ_BLK = 2
MXU_ROWS = 16


def _sample_seq_kernel(proj_ref, s_ref, sc_ref, cos_ref, sin_ref, gam_ref, cw_ref, cb_ref,
                       o_ref, c_ref, s_out_ref, sc_out_ref):
    cos = cos_ref[...]
    sin = sin_ref[...]
    u = proj_ref[0, :, O_GLU_A:O_GLU_A + D] * _sigmoid(proj_ref[0, :, O_GLU_B:O_GLU_B + D])
    row = lax.broadcasted_iota(jnp.int32, (MXU_ROWS, DK), 0)
    for s in range(SEQ_BLK):
        us = u[s:s + 1, :]
        c_ref[0, s:s + 1, :] = (
            jnp.sum(cw_ref[0:CONV_W - 1, :] * sc_ref[s], axis=0, keepdims=True)
            + cw_ref[CONV_W - 1:CONV_W, :] * us + cb_ref[...])
        sc_out_ref[s, 0:CONV_W - 2, :] = sc_ref[s, 1:CONV_W - 1, :]
        sc_out_ref[s, CONV_W - 2:CONV_W - 1, :] = us
        for h in range(N_HEADS):
            q = _rotary(proj_ref[0, s:s + 1, O_Q + h * DK:O_Q + (h + 1) * DK], cos, sin)
            k = _rotary(proj_ref[0, s:s + 1, O_K + h * DK:O_K + (h + 1) * DK], cos, sin) * (DK ** -0.5)
            v = proj_ref[0, s:s + 1, O_V + h * DV:O_V + (h + 1) * DV]
            gam = gam_ref[h]
            qk = jnp.sum(q * k, axis=-1, keepdims=True)
            s_old = s_ref[s, h]
            o8 = _bdot(jnp.broadcast_to(q * gam, (MXU_ROWS, DK)), s_old)
            o_ref[0, s:s + 1, h * DV:(h + 1) * DV] = qk * v + o8[0:1, :]
            k8 = jnp.where(row == 0, jnp.broadcast_to(k, (MXU_ROWS, DK)), 0.0).astype(BF)
            v8 = jnp.broadcast_to(v, (MXU_ROWS, DV)).astype(BF)
            kv = lax.dot_general(k8, v8, (((0,), (0,)), ((), ())), preferred_element_type=F32)
            s_out_ref[s, h] = gam * s_old + kv


def _sample_seq(proj_s, state_ret, state_conv, cos_s, sin_s, gam, cw, cb):
    n = proj_s.shape[0]
    nb = n // SEQ_BLK
    proj3 = proj_s.reshape(nb, SEQ_BLK, D_IN)
    z2 = lambda i: (0, 0)
    o, c, s_new, sc_new = pl.pallas_call(
        _sample_seq_kernel,
        grid=(nb,),
        in_specs=[
            pl.BlockSpec((1, SEQ_BLK, D_IN), lambda i: (i, 0, 0)),
            pl.BlockSpec((SEQ_BLK, N_HEADS, DK, DV), lambda i: (i, 0, 0, 0)),
            pl.BlockSpec((SEQ_BLK, CONV_W - 1, D), lambda i: (i, 0, 0)),
            pl.BlockSpec((1, LANES), z2),
            pl.BlockSpec((1, LANES), z2),
            pl.BlockSpec(memory_space=pltpu.SMEM),
            pl.BlockSpec((CONV_W, D), z2),
            pl.BlockSpec((1, D), z2),
        ],
        out_specs=[
            pl.BlockSpec((1, SEQ_BLK, D_V), lambda i: (i, 0, 0)),
            pl.BlockSpec((1, SEQ_BLK, D), lambda i: (i, 0, 0)),
            pl.BlockSpec((SEQ_BLK, N_HEADS, DK, DV), lambda i: (i, 0, 0, 0)),
            pl.BlockSpec((SEQ_BLK, CONV_W - 1, D), lambda i: (i, 0, 0)),
        ],
        out_shape=[
            jax.ShapeDtypeStruct((nb, SEQ_BLK, D_V), F32),
            jax.ShapeDtypeStruct((nb, SEQ_BLK, D), F32),
            jax.ShapeDtypeStruct(state_ret.shape, F32),
            jax.ShapeDtypeStruct(state_conv.shape, F32),
        ],
        compiler_params=pltpu.CompilerParams(
            dimension_semantics=("arbitrary",), vmem_limit_bytes=VMEM_LIMIT),
        name="sample_seq",
    )(proj3, state_ret, state_conv, cos_s, sin_s, gam, cw, cb)
    return o.reshape(n, D_V), c.reshape(n, D), s_new, sc_new


def _post_kernel(c_ref, o_ref, gret_ref, gates_ref, x_ref, *rest):
    p = dict(zip(_POST_NAMES, rest[:len(_POST_NAMES)]))
    x1_ref, comb_ref = rest[len(_POST_NAMES):]
    x1, comb, _ = _post_tokens(c_ref[...], o_ref[...], gret_ref[...], gates_ref[:, :D], gates_ref[:, D:],
                               x_ref[...], p)
    x1_ref[...] = x1
    comb_ref[...] = comb


def _post_call(c, o, proj_s, x_s, post_args):
    n = x_s.shape[0]
    z2 = lambda i: (0, 0)
    return pl.pallas_call(
        _post_kernel,
        grid=(1,),
        in_specs=[
            pl.BlockSpec((n, D), z2),
            pl.BlockSpec((n, D_V), z2),
            pl.BlockSpec((n, D_V), lambda i: (0, O_GRET // D_V)),
            pl.BlockSpec((n, 2 * D), lambda i: (0, O_GA // (2 * D))),
            pl.BlockSpec((n, D), z2),
        ] + _post_specs(1),
        out_specs=[pl.BlockSpec((n, D), z2), pl.BlockSpec((n, LANES), z2)],
        out_shape=[jax.ShapeDtypeStruct((n, D), F32), jax.ShapeDtypeStruct((n, LANES), F32)],
        compiler_params=pltpu.CompilerParams(
            dimension_semantics=("arbitrary",), vmem_limit_bytes=VMEM_LIMIT),
        name="sample_post",
    )(c, o, proj_s, proj_s, x_s, *post_args)


def _moe_kernel(x1_ref, comb_ref, fg_ref, ng_ref, wg_ref, wu_ref, wd_ref, y_ref, t_ref, acc_ref):
    e = pl.program_id(1)

    @pl.when(e == 0)
    def _():
        t_ref[...] = _rmsnorm(x1_ref[...], fg_ref[...]).astype(BF)
        acc_ref[...] = jnp.zeros_like(acc_ref)

    t = t_ref[...]
    hg = jnp.dot(t, wg_ref[0], preferred_element_type=F32)
    hu = jnp.dot(t, wu_ref[0], preferred_element_type=F32)
    comb = comb_ref[...]
    lane = lax.broadcasted_iota(jnp.int32, comb.shape, 1)
    ce = jnp.sum(jnp.where(lane == e, comb, 0.0), axis=-1, keepdims=True)
    act = (hg * _sigmoid(hg)) * hu * ce
    acc_ref[...] += jnp.dot(act.astype(BF), wd_ref[0], preferred_element_type=F32)

    @pl.when(e == N_EXPERTS - 1)
    def _():
        y_ref[...] = _rmsnorm(x1_ref[...] + acc_ref[...], ng_ref[...])


def _moe_call(x1, comb, ffn_g, final_g, wg, wu, wd, tm):
    n = x1.shape[0]
    z2 = lambda i, e: (0, 0)
    return pl.pallas_call(
        _moe_kernel,
        grid=(n // tm, N_EXPERTS),
        in_specs=[
            pl.BlockSpec((tm, D), lambda i, e: (i, 0)),
            pl.BlockSpec((tm, LANES), lambda i, e: (i, 0)),
            pl.BlockSpec((1, D), z2),
            pl.BlockSpec((1, D), z2),
            pl.BlockSpec((1, D, D_EXPERT), lambda i, e: (e, 0, 0)),
            pl.BlockSpec((1, D, D_EXPERT), lambda i, e: (e, 0, 0)),
            pl.BlockSpec((1, D_EXPERT, D), lambda i, e: (e, 0, 0)),
        ],
        out_specs=pl.BlockSpec((tm, D), lambda i, e: (i, 0)),
        out_shape=jax.ShapeDtypeStruct((n, D), F32),
        scratch_shapes=[pltpu.VMEM((tm, D), BF), pltpu.VMEM((tm, D), F32)],
        compiler_params=pltpu.CompilerParams(
            dimension_semantics=("arbitrary", "arbitrary"), vmem_limit_bytes=VMEM_LIMIT),
        name="moe",
    )(x1, comb, ffn_g, final_g, wg, wu, wd)


def _moe_sparse_kernel(slot_ref, e_lo_ref, e_hi_ref, nvalid_ref, nused_ref,
                       rows_hbm, fg_ref, ng_ref, wg_lo, wu_lo, wd_lo, wg_hi, wu_hi, wd_hi,
                       y_hbm,
                       inv_ref, xbuf, ybuf, gsem, ssem):
    t = pl.program_id(0)
    n_used = nused_ref[0]
    n_tok = slot_ref.shape[0]
    cur = lax.rem(t, 2)

    def start_gather(tile, buf):
        base = tile * MOE_TM
        last = nvalid_ref[tile] - 1

        def body(r, c):
            tok = inv_ref[base + jnp.minimum(r, last)]
            pltpu.make_async_copy(rows_hbm.at[pl.ds(tok, 1), :], xbuf.at[buf, pl.ds(r, 1), :],
                                  gsem.at[buf]).start()
            return c
        lax.fori_loop(0, MOE_TM, body, 0, unroll=DMA_UNROLL)

    def tile_gather_wait(buf):
        pltpu.make_async_copy(rows_hbm.at[pl.ds(0, MOE_TM), :], xbuf.at[buf], gsem.at[buf]).wait()

    def scatter_copy(r, tok, buf):
        return pltpu.make_async_copy(ybuf.at[buf, pl.ds(r, 1), :], y_hbm.at[pl.ds(tok, 1), :], ssem.at[buf])

    def tile_scatter_wait(buf, n_rows):
        k = MOE_TM
        while k >= 1:
            @pl.when((n_rows & k) != 0)
            def _(k=k):
                pltpu.make_async_copy(ybuf.at[buf, pl.ds(0, k), :], y_hbm.at[pl.ds(0, k), :],
                                      ssem.at[buf]).wait()
            k //= 2

    @pl.when(t == 0)
    def _():
        def fill(tok, c):
            inv_ref[slot_ref[tok]] = tok
            return c
        lax.fori_loop(0, n_tok, fill, 0, unroll=DMA_UNROLL)
        start_gather(0, 0)

    @pl.when(t < n_used)
    def _():
        tile_gather_wait(cur)

        @pl.when(t + 1 < n_used)
        def _():
            start_gather(t + 1, 1 - cur)

        x1 = xbuf[cur, :, 0:D]
        aux = xbuf[cur, :, D:ROW_W]
        tb = _rmsnorm(x1, fg_ref[...]).astype(BF)
        acc = None
        for wg, wu, wd, lane_w in ((wg_lo, wu_lo, wd_lo, AUX_W_LO), (wg_hi, wu_hi, wd_hi, AUX_W_HI)):
            hg = jnp.dot(tb, wg[0], preferred_element_type=F32)
            hu = jnp.dot(tb, wu[0], preferred_element_type=F32)
            act = (hg * _sigmoid(hg)) * hu * aux[:, lane_w:lane_w + 1]
            part = jnp.dot(act.astype(BF), wd[0], preferred_element_type=F32)
            acc = part if acc is None else acc + part
        y = _rmsnorm(x1 + acc, ng_ref[...])

        @pl.when(t >= 2)
        def _():
            tile_scatter_wait(cur, nvalid_ref[jnp.maximum(t - 2, 0)])
        ybuf[cur] = y

        base = t * MOE_TM
        n_rows = nvalid_ref[t]
        n_blk = n_rows // DMA_UNROLL

        def blk(b, c):
            for u in range(DMA_UNROLL):
                r = b * DMA_UNROLL + u
                scatter_copy(r, inv_ref[base + r], cur).start()
            return c
        lax.fori_loop(0, n_blk, blk, 0)

        def tail(r, c):
            scatter_copy(r, inv_ref[base + r], cur).start()
            return c
        lax.fori_loop(n_blk * DMA_UNROLL, n_rows, tail, 0)

        @pl.when(t == n_used - 1)
        def _():
            tile_scatter_wait(cur, n_rows)

            @pl.when(t >= 1)
            def _():
                tile_scatter_wait(1 - cur, nvalid_ref[jnp.maximum(t - 1, 0)])


def _bin_tables():
    lo, hi = [], []
    for g in range(N_GROUPS):
        for a in range(EPG):
            for b in range(a + 1, EPG):
                lo.append(g * EPG + a)
                hi.append(g * EPG + b)
    return jnp.array(lo, jnp.int32), jnp.array(hi, jnp.int32)


def _moe_sparse_call(rows, counts, ffn_g, final_g, wg, wu, wd):
    n = rows.shape[0]
    max_tiles = n // MOE_TM + N_BINS
    padded = ((counts + MOE_TM - 1) // MOE_TM) * MOE_TM
    ends = jnp.cumsum(padded)
    offs = ends - padded
    n_used = ends[-1] // MOE_TM
    bins = rows[:, D + AUX_BIN].astype(jnp.int32)
    ranks = rows[:, D + AUX_RANK].astype(jnp.int32)
    slot = offs[bins] + ranks
    tile_start = jnp.minimum(jnp.arange(max_tiles, dtype=jnp.int32), n_used - 1) * MOE_TM
    tile_bin = jnp.sum((tile_start[:, None] >= ends[None, :]).astype(jnp.int32), axis=1)
    bin_lo, bin_hi = _bin_tables()
    e_lo = bin_lo[tile_bin]
    e_hi = bin_hi[tile_bin]
    tile_id = jnp.arange(max_tiles, dtype=jnp.int32)
    n_valid = jnp.where(tile_id < n_used,
                        jnp.clip(counts[tile_bin] - (tile_start - offs[tile_bin]), 0, MOE_TM), 0)

    z2 = lambda t, *_: (0, 0)
    w_lo_map = lambda t, slot_r, lo_r, hi_r, nv_r, nu_r: (lo_r[t], 0, 0)
    w_hi_map = lambda t, slot_r, lo_r, hi_r, nv_r, nu_r: (hi_r[t], 0, 0)
    return pl.pallas_call(
        _moe_sparse_kernel,
        grid_spec=pltpu.PrefetchScalarGridSpec(
            num_scalar_prefetch=5,
            grid=(max_tiles,),
            in_specs=[
                pl.BlockSpec(memory_space=pl.ANY),
                pl.BlockSpec((1, D), z2),
                pl.BlockSpec((1, D), z2),
                pl.BlockSpec((1, D, D_EXPERT), w_lo_map),
                pl.BlockSpec((1, D, D_EXPERT), w_lo_map),
                pl.BlockSpec((1, D_EXPERT, D), w_lo_map),
                pl.BlockSpec((1, D, D_EXPERT), w_hi_map),
                pl.BlockSpec((1, D, D_EXPERT), w_hi_map),
                pl.BlockSpec((1, D_EXPERT, D), w_hi_map),
            ],
            out_specs=pl.BlockSpec(memory_space=pl.ANY),
            scratch_shapes=[
                pltpu.SMEM((max_tiles * MOE_TM,), jnp.int32),
                pltpu.VMEM((2, MOE_TM, ROW_W), F32),
                pltpu.VMEM((2, MOE_TM, D), F32),
                pltpu.SemaphoreType.DMA((2,)),
                pltpu.SemaphoreType.DMA((2,)),
            ],
        ),
        out_shape=jax.ShapeDtypeStruct((n, D), F32),
        compiler_params=pltpu.CompilerParams(
            dimension_semantics=("arbitrary",), vmem_limit_bytes=VMEM_LIMIT),
        name="moe_sparse",
    )(slot, e_lo, e_hi, n_valid, n_used.reshape(1).astype(jnp.int32),
      rows, ffn_g, final_g, wg, wu, wd, wg, wu, wd)


def _rope_tables(pos):
    half = DK // 2
    freqs = jnp.power(ROPE_BASE, -jnp.arange(half, dtype=F32) / half)
    ang = pos.astype(F32)[:, None] * freqs[None, :]
    return jnp.cos(ang), jnp.sin(ang)


def _decay_tables(L):
    lg = jnp.log(1.0 - jnp.exp2(-5.0 - jnp.arange(N_HEADS, dtype=F32)))
    idx = jnp.arange(L, dtype=F32)
    diff = idx[:, None] - idx[None, :]
    dmat = jnp.where((diff >= 0)[None], jnp.exp(jnp.maximum(diff, 0.0)[None] * lg[:, None, None]), 0.0)
    dq = jnp.exp((idx[:, None] + 1.0) * lg[None, :]).T
    dk = jnp.exp((L - 1.0 - idx)[:, None] * lg[None, :]).T
    gl = jnp.exp(L * lg)
    bl = lambda a: jnp.broadcast_to(a[:, :, None], (N_HEADS, L, LANES))
    return dmat, bl(dq), bl(dk), gl


def kernel(x_prompt, x_sample, state_ret, state_conv, meta_tokens, norm_mix_g, w_in, b_gates, conv_w, conv_b, conv_ln_g, conv_ln_b, w_conv_out, ret_gn_g, w_ret_out, w_o, norm_ffn_g, w_coarse, b_coarse, w_fine, b_fine, w_gate_e, w_up_e, w_down_e, norm_final_g):
    bp, seq, _ = x_prompt.shape
    ns = x_sample.shape[0]
    l = 0
    row = lambda a: a.reshape(1, -1)

    w_in_bf = w_in[l].astype(BF)
    wr = jnp.zeros((D, LANES), F32).at[:, :N_EXPERTS].set(w_fine[l]).at[:, N_EXPERTS:N_EXPERTS + N_GROUPS].set(w_coarse[l])
    wr_hi = wr.astype(BF)
    wr_lo = (wr - wr_hi.astype(F32)).astype(BF)
    br = jnp.zeros((1, LANES), F32).at[0, :N_EXPERTS].set(b_fine[l]).at[0, N_EXPERTS:N_EXPERTS + N_GROUPS].set(b_coarse[l])
    post_args = (row(conv_ln_g[l]), row(conv_ln_b[l]), w_conv_out[l].astype(BF), row(ret_gn_g[l]),
                 w_ret_out[l].astype(BF), row(b_gates[l]), w_o[l].astype(BF), row(norm_ffn_g[l]),
                 wr_hi, wr_lo, br)
    wg = w_gate_e[l].astype(BF)
    wu = w_up_e[l].astype(BF)
    wd = w_down_e[l].astype(BF)
    mix_g = row(norm_mix_g[l])
    cw = conv_w[l]
    cb = row(conv_b[l])

    tp = PAD_FRONT + N_META + seq
    xpad = jnp.concatenate([
        jnp.zeros((bp, PAD_FRONT, D), F32),
        jnp.broadcast_to(meta_tokens[None], (bp, N_META, D)),
        x_prompt], axis=1)
    proj_p = _proj_call(xpad.reshape(bp * tp, D), mix_g, w_in_bf, tm=256).reshape(bp, tp, D_IN)
    pos_p = jnp.maximum(jnp.arange(tp, dtype=F32) - PAD_FRONT, 0.0)
    cos_p, sin_p = _rope_tables(pos_p)
    dmat, dq, dk, gl = _decay_tables(TILE)
    rows_p, counts, s_new_p, buf_p = _prompt_mix(xpad, proj_p, (cos_p, sin_p, dmat, dq, dk, gl), cw, cb, post_args)
    y_p = _moe_sparse_call(rows_p.reshape(bp * seq, ROW_W), counts[0, :N_BINS].astype(jnp.int32),
                           row(norm_ffn_g[l]), row(norm_final_g), wg, wu, wd).reshape(bp, seq, D)

    xs = x_sample.reshape(ns, D)
    proj_s = _proj_call(xs, mix_g, w_in_bf, tm=ns)
    pos_s = jnp.full((1,), float(PAST_LEN), F32)
    cos_s, sin_s = _rope_tables(pos_s)
    _, _, _, gam = _decay_tables(1)
    o_s, c_s, s_new_s, buf_s = _sample_seq(proj_s, state_ret[l], state_conv[l], cos_s, sin_s, gam, cw, cb)
    x1_s, comb_s = _post_call(c_s, o_s, proj_s, xs, post_args)
    y_s = _moe_call(x1_s, comb_s, row(norm_ffn_g[l]), row(norm_final_g), wg, wu, wd, tm=ns).reshape(ns, 1, D)

    return (y_p, y_s, s_new_p[None], buf_p[:, HALO - (CONV_W - 1):][None], s_new_s[None], buf_s[None])
```

```python
import jax
import jax.numpy as jnp
from jax import lax
from jax.experimental import pallas as pl
from jax.experimental.pallas import tpu as pltpu

D = 1024
N_META = 16
PAST_LEN = 16384
CONV_W = 31
N_HEADS = 4
DK = 256
DV = 512
D_QK = N_HEADS * DK
D_V = N_HEADS * DV
ROPE_BASE = 10000.0
N_GROUPS = 4
EPG = 4
N_EXPERTS = 16
D_EXPERT = 512
EPS = 1e-6
D_IN = 2 * D + 2 * D_QK + 2 * D_V + 2 * D

O_GLU_A, O_GLU_B = 0, D
O_Q, O_K = 2 * D, 2 * D + D_QK
O_V = 2 * D + 2 * D_QK
O_GRET = O_V + D_V
O_GA = O_GRET + D_V
O_GB = O_GA + D

N_PAIRS = EPG * (EPG - 1) // 2
N_BINS = N_GROUPS * N_PAIRS

LANES = 128
SUBLANES = 8
MXU_DIM = 256
ROW_W = D + LANES
AUX_W_LO, AUX_W_HI = 0, 1
ROUTE_BIN, ROUTE_RANK = 0, 1
MOE_TM = MXU_DIM
DMA_UNROLL = 8
PT = MXU_DIM
CONV_COLS = MXU_DIM
HALO = 32
VMEM_LIMIT = 56 * 1024 * 1024

BF = jnp.bfloat16
F32 = jnp.float32


def _sigmoid(x):
    return 1.0 / (1.0 + jnp.exp(-x))


def _bdot(a, b):
    return jnp.dot(a.astype(BF), b.astype(BF), preferred_element_type=F32)


def _rmsnorm(x, g):
    return x * lax.rsqrt(jnp.mean(x * x, axis=-1, keepdims=True) + EPS) * g


def _proj_kernel(x_ref, g_ref, w_ref, o_ref):
    h = _rmsnorm(x_ref[...], g_ref[...])
    o_ref[...] = jnp.dot(h.astype(BF), w_ref[...], preferred_element_type=F32)


def _proj_call(x2d, g, w_bf, tm, tn=2048):
    n = x2d.shape[0]
    return pl.pallas_call(
        _proj_kernel,
        grid=(D_IN // tn, n // tm),
        in_specs=[
            pl.BlockSpec((tm, D), lambda j, i: (i, 0)),
            pl.BlockSpec((1, D), lambda j, i: (0, 0)),
            pl.BlockSpec((D, tn), lambda j, i: (0, j)),
        ],
        out_specs=pl.BlockSpec((tm, tn), lambda j, i: (i, j)),
        out_shape=jax.ShapeDtypeStruct((n, D_IN), F32),
        compiler_params=pltpu.CompilerParams(
            dimension_semantics=("arbitrary", "arbitrary"), vmem_limit_bytes=VMEM_LIMIT),
        name="in_proj",
    )(x2d, g, w_bf)


def _route(t, wr_hi_ref, wr_lo_ref, br_ref):
    t_hi = t.astype(BF)
    t_lo = (t - t_hi.astype(F32)).astype(BF)
    wh = wr_hi_ref[...]
    logits = (jnp.dot(t_hi, wh, preferred_element_type=F32)
              + jnp.dot(t_hi, wr_lo_ref[...], preferred_element_type=F32)
              + jnp.dot(t_lo, wh, preferred_element_type=F32)) + br_ref[...]
    lane = lax.broadcasted_iota(jnp.int32, logits.shape, 1)
    neg = jnp.float32(-jnp.inf)
    is_c = (lane >= N_EXPERTS) & (lane < N_EXPERTS + N_GROUPS)
    lc = jnp.where(is_c, logits, neg)
    cmax = jnp.max(lc, axis=-1, keepdims=True)
    csum = jnp.sum(jnp.where(is_c, jnp.exp(lc - cmax), 0.0), axis=-1, keepdims=True)
    p_g = 1.0 / csum
    g_sel = jnp.min(jnp.where(is_c & (lc == cmax), lane - N_EXPERTS, 1 << 20), axis=-1, keepdims=True)
    in_g = (lane < N_EXPERTS) & ((lane >> 2) == g_sel)
    f1 = jnp.where(in_g, logits, neg)
    v1 = jnp.max(f1, axis=-1, keepdims=True)
    i1 = jnp.min(jnp.where(in_g & (f1 == v1), lane, 1 << 20), axis=-1, keepdims=True)
    rest = in_g & (lane != i1)
    f2 = jnp.where(rest, logits, neg)
    v2 = jnp.max(f2, axis=-1, keepdims=True)
    i2 = jnp.min(jnp.where(rest & (f2 == v2), lane, 1 << 20), axis=-1, keepdims=True)
    e2 = jnp.exp(v2 - v1)
    w1 = p_g / (1.0 + e2)
    w2 = e2 * w1
    comb = jnp.where(lane == i1, w1, jnp.where(lane == i2, w2, 0.0))
    first_low = i1 < i2
    a = jnp.where(first_low, i1, i2) & (EPG - 1)
    b = jnp.where(first_low, i2, i1) & (EPG - 1)
    pair = jnp.where(a == 0, 0, jnp.where(a == 1, 3, 5)) + (b - a - 1)
    bin_id = g_sel * N_PAIRS + pair
    return comb, (jnp.where(first_low, w1, w2), jnp.where(first_low, w2, w1), bin_id)


def _post_tokens(c, o, g_ret, gate_a, gate_b, x, p):
    mu = jnp.mean(c, axis=-1, keepdims=True)
    dc = c - mu
    var = jnp.mean(dc * dc, axis=-1, keepdims=True)
    cn = dc * lax.rsqrt(var + EPS) * p["ln_g"][...] + p["ln_b"][...]
    y_a = _bdot(cn * _sigmoid(cn), p["w_conv_out"][...])
    parts = []
    for h in range(N_HEADS):
        oh = o[:, h * DV:(h + 1) * DV]
        omu = jnp.mean(oh, axis=-1, keepdims=True)
        od = oh - omu
        ovar = jnp.mean(od * od, axis=-1, keepdims=True)
        parts.append(od * lax.rsqrt(ovar + EPS))
    on = jnp.concatenate(parts, axis=-1) * p["gn_g"][...]
    y_b = _bdot(on * (g_ret * _sigmoid(g_ret)), p["w_ret_out"][...])
    bg = p["b_gates"][...]
    g_a = _sigmoid(gate_a + bg[:, :D])
    g_b = _sigmoid(gate_b + bg[:, D:])
    x1 = x + _bdot(g_a * y_a + g_b * y_b, p["w_o"][...])
    t = _rmsnorm(x1, p["ffn_g"][...])
    comb, pair_route = _route(t, p["wr_hi"], p["wr_lo"], p["br"])
    return x1, comb, pair_route


_POST_NAMES = ("ln_g", "ln_b", "w_conv_out", "gn_g", "w_ret_out", "b_gates", "w_o", "ffn_g",
               "wr_hi", "wr_lo", "br")


def _post_specs(nidx):
    z2 = lambda *_: (0, 0)
    shapes = {"ln_g": (1, D), "ln_b": (1, D), "w_conv_out": (D, D), "gn_g": (1, D_V),
              "w_ret_out": (D_V, D), "b_gates": (1, 2 * D), "w_o": (D, D), "ffn_g": (1, D),
              "wr_hi": (D, LANES), "wr_lo": (D, LANES), "br": (1, LANES)}
    return [pl.BlockSpec(shapes[k], z2) for k in _POST_NAMES]


def _rotary(x, cos, sin):
    x1, x2 = x[:, :LANES], x[:, LANES:]
    return jnp.concatenate([x1 * cos - x2 * sin, x1 * sin + x2 * cos], axis=-1)


_SMALL_NAMES = ("ln_g", "ln_b", "gn_g", "b_gates", "ffn_g", "wr_hi", "wr_lo", "br")
_BIG_NAMES = ("w_in", "w_conv_out", "w_ret_out", "w_o")


def _prompt_kernel(x_ref, buf0_ref, cos_ref, sin_ref, dmat_ref, dq_ref, dk_ref, gl_ref,
                   cw_ref, cb_ref, tri_ref, mixg_ref, *rest):
    ns, nb = len(_SMALL_NAMES), len(_BIG_NAMES)
    small = dict(zip(_SMALL_NAMES, rest[:ns]))
    s0_hbm = rest[ns]
    big_hbm = rest[ns + 1:ns + 1 + nb]
    x1e_ref, route_ref, cnt_out_ref, s_out_hbm, buf_out_ref = rest[ns + 1 + nb:ns + 6 + nb]
    scratch = rest[ns + 6 + nb:]
    big = dict(zip(_BIG_NAMES, scratch[:nb]))
    wsem, ssem, shift_ref, tail_ref, s_ref, c_ref, o_ref, h_ref, cnt_ref = scratch[nb:]
    p = dict(small, **{k: big[k] for k in _BIG_NAMES[1:]})
    w_in = big["w_in"]
    b = pl.program_id(0)
    i = pl.program_id(1)

    @pl.when((b == 0) & (i == 0))
    def _():
        copies = [pltpu.make_async_copy(src, big[k], wsem.at[n])
                  for n, (k, src) in enumerate(zip(_BIG_NAMES, big_hbm))]
        for cp in copies:
            cp.start()
        for cp in copies:
            cp.wait()
        cnt_ref[...] = jnp.zeros_like(cnt_ref)

    @pl.when(i == 0)
    def _():
        load_state = pltpu.make_async_copy(s0_hbm, s_ref, ssem)
        load_state.start()
        load_state.wait()
        tail_ref[...] = buf0_ref[...]

    h_ref[...] = _rmsnorm(x_ref[0], mixg_ref[...]).astype(BF)

    def proj(lo, width):
        return jnp.dot(h_ref[...], w_in[:, lo:lo + width], preferred_element_type=F32)

    base = HALO - (CONV_W - 1)
    for cb in range(D // CONV_COLS):
        c0 = cb * CONV_COLS
        u = proj(O_GLU_A + c0, CONV_COLS) * _sigmoid(proj(O_GLU_B + c0, CONV_COLS))
        for s in range(SUBLANES):
            shift_ref[s, 0:HALO - s, :] = tail_ref[s:HALO, c0:c0 + CONV_COLS]
            shift_ref[s, HALO - s:HALO - s + PT, :] = u
        tail_ref[:, c0:c0 + CONV_COLS] = u[PT - HALO:PT, :]
        for half in range(CONV_COLS // LANES):
            lanes = slice(half * LANES, (half + 1) * LANES)
            cols = slice(c0 + half * LANES, c0 + (half + 1) * LANES)
            acc = jnp.broadcast_to(cb_ref[:, cols], (PT, LANES))
            for j in range(CONV_W):
                s = (base + j) % SUBLANES
                start = base + j - s
                acc = acc + cw_ref[j:j + 1, cols] * shift_ref[s, start:start + PT, lanes]
            c_ref[:, cols] = acc

    cos = cos_ref[...]
    sin = sin_ref[...]
    for h in range(N_HEADS):
        q = _rotary(proj(O_Q + h * DK, DK), cos, sin)
        k = _rotary(proj(O_K + h * DK, DK), cos, sin) * (DK ** -0.5)
        v = proj(O_V + h * DV, DV).astype(BF)
        dq = dq_ref[h]
        dk = dk_ref[h]
        scores = lax.dot_general(q.astype(BF), k.astype(BF), (((1,), (1,)), ((), ())),
                                 preferred_element_type=F32) * dmat_ref[h]
        s_old = s_ref[h]
        q_dec = jnp.concatenate([q[:, :LANES] * dq, q[:, LANES:] * dq], axis=-1)
        k_dec = jnp.concatenate([k[:, :LANES] * dk, k[:, LANES:] * dk], axis=-1)
        o_ref[:, h * DV:(h + 1) * DV] = (jnp.dot(scores.astype(BF), v, preferred_element_type=F32)
                                         + _bdot(q_dec, s_old))
        s_ref[h] = gl_ref[h] * s_old + lax.dot_general(
            k_dec.astype(BF), v, (((0,), (0,)), ((), ())), preferred_element_type=F32)

    @pl.when(i == pl.num_programs(1) - 1)
    def _():
        store_state = pltpu.make_async_copy(s_ref, s_out_hbm.at[b], ssem)
        store_state.start()
        store_state.wait()
        buf_out_ref[0] = tail_ref[...]

    x1, _, (w_lo, w_hi, bin_id) = _post_tokens(c_ref[...], o_ref[...], proj(O_GRET, D_V),
                                               proj(O_GA, D), proj(O_GB, D), x_ref[0], p)

    lane = lax.broadcasted_iota(jnp.int32, (PT, LANES), 1)
    onehot = lane == bin_id
    earlier = jnp.dot(tri_ref[...], onehot.astype(BF), preferred_element_type=F32)
    rank = jnp.sum(jnp.where(onehot, earlier + cnt_ref[...], 0.0), axis=-1, keepdims=True)
    cnt_ref[...] += jnp.sum(onehot.astype(F32), axis=0, keepdims=True)

    x1e_ref[0, :, 0:D] = x1
    x1e_ref[0, :, D:ROW_W] = jnp.where(lane == AUX_W_LO, w_lo, jnp.where(lane == AUX_W_HI, w_hi, 0.0))
    route_ref[0] = jnp.where(lane == ROUTE_BIN, bin_id,
                             jnp.where(lane == ROUTE_RANK, rank.astype(jnp.int32), 0))
    cnt_out_ref[...] = cnt_ref[...]


def _prompt_call(x, s0, buf0, tabs, consts, small, big, name):
    b, t, _ = x.shape
    cos, sin, dmat, dq, dk, gl = tabs
    z2 = lambda bi, i: (0, 0)
    z3 = lambda bi, i: (0, 0, 0)
    tok = lambda bi, i: (bi, i, 0)
    whole = lambda a: pl.BlockSpec(a.shape, z2 if a.ndim == 2 else z3)
    any_spec = pl.BlockSpec(memory_space=pl.ANY)
    return pl.pallas_call(
        _prompt_kernel,
        grid=(b, t // PT),
        in_specs=[
            pl.BlockSpec((1, PT, D), tok),
            whole(buf0),
            pl.BlockSpec((PT, LANES), lambda bi, i: (i, 0)),
            pl.BlockSpec((PT, LANES), lambda bi, i: (i, 0)),
            whole(dmat), whole(dq), whole(dk),
            pl.BlockSpec(memory_space=pltpu.SMEM),
        ] + [whole(a) for a in consts] + [whole(a) for a in small] + [any_spec] * (1 + len(big)),
        out_specs=[
            pl.BlockSpec((1, PT, ROW_W), tok),
            pl.BlockSpec((1, PT, LANES), tok),
            pl.BlockSpec((1, LANES), z2),
            any_spec,
            pl.BlockSpec((1, HALO, D), lambda bi, i: (bi, 0, 0)),
        ],
        out_shape=[
            jax.ShapeDtypeStruct((b, t, ROW_W), F32),
            jax.ShapeDtypeStruct((b, t, LANES), jnp.int32),
            jax.ShapeDtypeStruct((1, LANES), F32),
            jax.ShapeDtypeStruct((b, N_HEADS, DK, DV), F32),
            jax.ShapeDtypeStruct((b, HALO, D), F32),
        ],
        scratch_shapes=[pltpu.VMEM(w.shape, BF) for w in big] + [
            pltpu.SemaphoreType.DMA((len(big),)),
            pltpu.SemaphoreType.DMA,
            pltpu.VMEM((SUBLANES, HALO + PT, CONV_COLS), F32),
            pltpu.VMEM((HALO, D), F32),
            pltpu.VMEM((N_HEADS, DK, DV), F32),
            pltpu.VMEM((PT, D), F32),
            pltpu.VMEM((PT, D_V), F32),
            pltpu.VMEM((PT, D), BF),
            pltpu.VMEM((1, LANES), F32),
        ],
        compiler_params=pltpu.CompilerParams(
            dimension_semantics=("arbitrary", "arbitrary"), vmem_limit_bytes=VMEM_LIMIT),
        name=name,
    )(x, buf0, cos, sin, dmat, dq, dk, gl, *consts, *small, s0, *big)


SEQ_BLK = 2
MXU_ROWS = 16


def _sample_seq_kernel(proj_ref, s_ref, sc_ref, cos_ref, sin_ref, gam_ref, cw_ref, cb_ref,
                       o_ref, c_ref, s_out_ref, sc_out_ref):
    cos = cos_ref[...]
    sin = sin_ref[...]
    u = proj_ref[0, :, O_GLU_A:O_GLU_A + D] * _sigmoid(proj_ref[0, :, O_GLU_B:O_GLU_B + D])
    row = lax.broadcasted_iota(jnp.int32, (MXU_ROWS, DK), 0)
    for s in range(SEQ_BLK):
        us = u[s:s + 1, :]
        c_ref[0, s:s + 1, :] = (
            jnp.sum(cw_ref[0:CONV_W - 1, :] * sc_ref[s], axis=0, keepdims=True)
            + cw_ref[CONV_W - 1:CONV_W, :] * us + cb_ref[...])
        sc_out_ref[s, 0:CONV_W - 2, :] = sc_ref[s, 1:CONV_W - 1, :]
        sc_out_ref[s, CONV_W - 2:CONV_W - 1, :] = us
        for h in range(N_HEADS):
            q = _rotary(proj_ref[0, s:s + 1, O_Q + h * DK:O_Q + (h + 1) * DK], cos, sin)
            k = _rotary(proj_ref[0, s:s + 1, O_K + h * DK:O_K + (h + 1) * DK], cos, sin) * (DK ** -0.5)
            v = proj_ref[0, s:s + 1, O_V + h * DV:O_V + (h + 1) * DV]
            gam = gam_ref[h]
            qk = jnp.sum(q * k, axis=-1, keepdims=True)
            s_old = s_ref[s, h]
            o8 = _bdot(jnp.broadcast_to(q * gam, (MXU_ROWS, DK)), s_old)
            o_ref[0, s:s + 1, h * DV:(h + 1) * DV] = qk * v + o8[0:1, :]
            k8 = jnp.where(row == 0, jnp.broadcast_to(k, (MXU_ROWS, DK)), 0.0).astype(BF)
            v8 = jnp.broadcast_to(v, (MXU_ROWS, DV)).astype(BF)
            kv = lax.dot_general(k8, v8, (((0,), (0,)), ((), ())), preferred_element_type=F32)
            s_out_ref[s, h] = gam * s_old + kv


def _sample_seq(proj_s, state_ret, state_conv, cos_s, sin_s, gam, cw, cb):
    n = proj_s.shape[0]
    nb = n // SEQ_BLK
    proj3 = proj_s.reshape(nb, SEQ_BLK, D_IN)
    z2 = lambda i: (0, 0)
    o, c, s_new, sc_new = pl.pallas_call(
        _sample_seq_kernel,
        grid=(nb,),
        in_specs=[
            pl.BlockSpec((1, SEQ_BLK, D_IN), lambda i: (i, 0, 0)),
            pl.BlockSpec((SEQ_BLK, N_HEADS, DK, DV), lambda i: (i, 0, 0, 0)),
            pl.BlockSpec((SEQ_BLK, CONV_W - 1, D), lambda i: (i, 0, 0)),
            pl.BlockSpec((1, LANES), z2),
            pl.BlockSpec((1, LANES), z2),
            pl.BlockSpec(memory_space=pltpu.SMEM),
            pl.BlockSpec((CONV_W, D), z2),
            pl.BlockSpec((1, D), z2),
        ],
        out_specs=[
            pl.BlockSpec((1, SEQ_BLK, D_V), lambda i: (i, 0, 0)),
            pl.BlockSpec((1, SEQ_BLK, D), lambda i: (i, 0, 0)),
            pl.BlockSpec((SEQ_BLK, N_HEADS, DK, DV), lambda i: (i, 0, 0, 0)),
            pl.BlockSpec((SEQ_BLK, CONV_W - 1, D), lambda i: (i, 0, 0)),
        ],
        out_shape=[
            jax.ShapeDtypeStruct((nb, SEQ_BLK, D_V), F32),
            jax.ShapeDtypeStruct((nb, SEQ_BLK, D), F32),
            jax.ShapeDtypeStruct(state_ret.shape, F32),
            jax.ShapeDtypeStruct(state_conv.shape, F32),
        ],
        compiler_params=pltpu.CompilerParams(
            dimension_semantics=("arbitrary",), vmem_limit_bytes=VMEM_LIMIT),
        name="sample_seq",
    )(proj3, state_ret, state_conv, cos_s, sin_s, gam, cw, cb)
    return o.reshape(n, D_V), c.reshape(n, D), s_new, sc_new


def _post_kernel(c_ref, o_ref, gret_ref, gates_ref, x_ref, *rest):
    p = dict(zip(_POST_NAMES, rest[:len(_POST_NAMES)]))
    x1_ref, comb_ref = rest[len(_POST_NAMES):]
    x1, comb, _ = _post_tokens(c_ref[...], o_ref[...], gret_ref[...], gates_ref[:, :D], gates_ref[:, D:],
                               x_ref[...], p)
    x1_ref[...] = x1
    comb_ref[...] = comb


def _post_call(c, o, proj_s, x_s, post_args):
    n = x_s.shape[0]
    z2 = lambda i: (0, 0)
    return pl.pallas_call(
        _post_kernel,
        grid=(1,),
        in_specs=[
            pl.BlockSpec((n, D), z2),
            pl.BlockSpec((n, D_V), z2),
            pl.BlockSpec((n, D_V), lambda i: (0, O_GRET // D_V)),
            pl.BlockSpec((n, 2 * D), lambda i: (0, O_GA // (2 * D))),
            pl.BlockSpec((n, D), z2),
        ] + _post_specs(1),
        out_specs=[pl.BlockSpec((n, D), z2), pl.BlockSpec((n, LANES), z2)],
        out_shape=[jax.ShapeDtypeStruct((n, D), F32), jax.ShapeDtypeStruct((n, LANES), F32)],
        compiler_params=pltpu.CompilerParams(
            dimension_semantics=("arbitrary",), vmem_limit_bytes=VMEM_LIMIT),
        name="sample_post",
    )(c, o, proj_s, proj_s, x_s, *post_args)


def _moe_kernel(x1_ref, comb_ref, fg_ref, ng_ref, wg_ref, wu_ref, wd_ref, y_ref, t_ref, acc_ref):
    e = pl.program_id(1)

    @pl.when(e == 0)
    def _():
        t_ref[...] = _rmsnorm(x1_ref[...], fg_ref[...]).astype(BF)
        acc_ref[...] = jnp.zeros_like(acc_ref)

    t = t_ref[...]
    hg = jnp.dot(t, wg_ref[0], preferred_element_type=F32)
    hu = jnp.dot(t, wu_ref[0], preferred_element_type=F32)
    comb = comb_ref[...]
    lane = lax.broadcasted_iota(jnp.int32, comb.shape, 1)
    ce = jnp.sum(jnp.where(lane == e, comb, 0.0), axis=-1, keepdims=True)
    act = (hg * _sigmoid(hg)) * hu * ce
    acc_ref[...] += jnp.dot(act.astype(BF), wd_ref[0], preferred_element_type=F32)

    @pl.when(e == N_EXPERTS - 1)
    def _():
        y_ref[...] = _rmsnorm(x1_ref[...] + acc_ref[...], ng_ref[...])


def _moe_call(x1, comb, ffn_g, final_g, wg, wu, wd, tm):
    n = x1.shape[0]
    z2 = lambda i, e: (0, 0)
    return pl.pallas_call(
        _moe_kernel,
        grid=(n // tm, N_EXPERTS),
        in_specs=[
            pl.BlockSpec((tm, D), lambda i, e: (i, 0)),
            pl.BlockSpec((tm, LANES), lambda i, e: (i, 0)),
            pl.BlockSpec((1, D), z2),
            pl.BlockSpec((1, D), z2),
            pl.BlockSpec((1, D, D_EXPERT), lambda i, e: (e, 0, 0)),
            pl.BlockSpec((1, D, D_EXPERT), lambda i, e: (e, 0, 0)),
            pl.BlockSpec((1, D_EXPERT, D), lambda i, e: (e, 0, 0)),
        ],
        out_specs=pl.BlockSpec((tm, D), lambda i, e: (i, 0)),
        out_shape=jax.ShapeDtypeStruct((n, D), F32),
        scratch_shapes=[pltpu.VMEM((tm, D), BF), pltpu.VMEM((tm, D), F32)],
        compiler_params=pltpu.CompilerParams(
            dimension_semantics=("arbitrary", "arbitrary"), vmem_limit_bytes=VMEM_LIMIT),
        name="moe",
    )(x1, comb, ffn_g, final_g, wg, wu, wd)


def _moe_sparse_kernel(slot_ref, e_lo_ref, e_hi_ref, nvalid_ref, nused_ref,
                       rows_hbm, fg_ref, ng_ref, wg_lo, wu_lo, wd_lo, wg_hi, wu_hi, wd_hi,
                       y_hbm,
                       inv_ref, xbuf, ybuf, gsem, ssem):
    t = pl.program_id(0)
    n_used = nused_ref[0]
    n_tok = slot_ref.shape[0]
    cur = lax.rem(t, 2)

    def start_gather(tile, buf):
        base = tile * MOE_TM
        last = nvalid_ref[tile] - 1

        def body(r, c):
            tok = inv_ref[base + jnp.minimum(r, last)]
            pltpu.make_async_copy(rows_hbm.at[pl.ds(tok, 1), :], xbuf.at[buf, pl.ds(r, 1), :],
                                  gsem.at[buf]).start()
            return c
        lax.fori_loop(0, MOE_TM, body, 0, unroll=DMA_UNROLL)

    def tile_gather_wait(buf):
        pltpu.make_async_copy(rows_hbm.at[pl.ds(0, MOE_TM), :], xbuf.at[buf], gsem.at[buf]).wait()

    def scatter_copy(r, tok, buf):
        return pltpu.make_async_copy(ybuf.at[buf, pl.ds(r, 1), :], y_hbm.at[pl.ds(tok, 1), :], ssem.at[buf])

    def tile_scatter_wait(buf, n_rows):
        k = MOE_TM
        while k >= 1:
            @pl.when((n_rows & k) != 0)
            def _(k=k):
                pltpu.make_async_copy(ybuf.at[buf, pl.ds(0, k), :], y_hbm.at[pl.ds(0, k), :],
                                      ssem.at[buf]).wait()
            k //= 2

    @pl.when(t == 0)
    def _():
        def fill(tok, c):
            inv_ref[slot_ref[tok]] = tok
            return c
        lax.fori_loop(0, n_tok, fill, 0, unroll=DMA_UNROLL)
        start_gather(0, 0)

    @pl.when(t < n_used)
    def _():
        tile_gather_wait(cur)

        @pl.when(t + 1 < n_used)
        def _():
            start_gather(t + 1, 1 - cur)

        x1 = xbuf[cur, :, 0:D]
        aux = xbuf[cur, :, D:ROW_W]
        tb = _rmsnorm(x1, fg_ref[...]).astype(BF)
        acc = None
        for wg, wu, wd, lane_w in ((wg_lo, wu_lo, wd_lo, AUX_W_LO), (wg_hi, wu_hi, wd_hi, AUX_W_HI)):
            hg = jnp.dot(tb, wg[0], preferred_element_type=F32)
            hu = jnp.dot(tb, wu[0], preferred_element_type=F32)
            act = (hg * _sigmoid(hg)) * hu * aux[:, lane_w:lane_w + 1]
            part = jnp.dot(act.astype(BF), wd[0], preferred_element_type=F32)
            acc = part if acc is None else acc + part
        y = _rmsnorm(x1 + acc, ng_ref[...])

        @pl.when(t >= 2)
        def _():
            tile_scatter_wait(cur, nvalid_ref[jnp.maximum(t - 2, 0)])
        ybuf[cur] = y

        base = t * MOE_TM
        n_rows = nvalid_ref[t]
        n_blk = n_rows // DMA_UNROLL

        def blk(b, c):
            for u in range(DMA_UNROLL):
                r = b * DMA_UNROLL + u
                scatter_copy(r, inv_ref[base + r], cur).start()
            return c
        lax.fori_loop(0, n_blk, blk, 0)

        def tail(r, c):
            scatter_copy(r, inv_ref[base + r], cur).start()
            return c
        lax.fori_loop(n_blk * DMA_UNROLL, n_rows, tail, 0)

        @pl.when(t == n_used - 1)
        def _():
            tile_scatter_wait(cur, n_rows)

            @pl.when(t >= 1)
            def _():
                tile_scatter_wait(1 - cur, nvalid_ref[jnp.maximum(t - 1, 0)])


def _bin_tables():
    lo, hi = [], []
    for g in range(N_GROUPS):
        for a in range(EPG):
            for b in range(a + 1, EPG):
                lo.append(g * EPG + a)
                hi.append(g * EPG + b)
    return jnp.array(lo, jnp.int32), jnp.array(hi, jnp.int32)


def _moe_sparse_call(rows, route, counts, ffn_g, final_g, wg, wu, wd):
    n = rows.shape[0]
    max_tiles = n // MOE_TM + N_BINS
    padded = ((counts + MOE_TM - 1) // MOE_TM) * MOE_TM
    ends = jnp.cumsum(padded)
    offs = ends - padded
    n_used = ends[-1] // MOE_TM
    slot = offs[route[:, ROUTE_BIN]] + route[:, ROUTE_RANK]
    tile_start = jnp.minimum(jnp.arange(max_tiles, dtype=jnp.int32), n_used - 1) * MOE_TM
    tile_bin = jnp.sum((tile_start[:, None] >= ends[None, :]).astype(jnp.int32), axis=1)
    bin_lo, bin_hi = _bin_tables()
    e_lo = bin_lo[tile_bin]
    e_hi = bin_hi[tile_bin]
    tile_id = jnp.arange(max_tiles, dtype=jnp.int32)
    n_valid = jnp.where(tile_id < n_used,
                        jnp.clip(counts[tile_bin] - (tile_start - offs[tile_bin]), 0, MOE_TM), 0)

    z2 = lambda t, *_: (0, 0)
    w_lo_map = lambda t, slot_r, lo_r, hi_r, nv_r, nu_r: (lo_r[t], 0, 0)
    w_hi_map = lambda t, slot_r, lo_r, hi_r, nv_r, nu_r: (hi_r[t], 0, 0)
    return pl.pallas_call(
        _moe_sparse_kernel,
        grid_spec=pltpu.PrefetchScalarGridSpec(
            num_scalar_prefetch=5,
            grid=(max_tiles,),
            in_specs=[
                pl.BlockSpec(memory_space=pl.ANY),
                pl.BlockSpec((1, D), z2),
                pl.BlockSpec((1, D), z2),
                pl.BlockSpec((1, D, D_EXPERT), w_lo_map),
                pl.BlockSpec((1, D, D_EXPERT), w_lo_map),
                pl.BlockSpec((1, D_EXPERT, D), w_lo_map),
                pl.BlockSpec((1, D, D_EXPERT), w_hi_map),
                pl.BlockSpec((1, D, D_EXPERT), w_hi_map),
                pl.BlockSpec((1, D_EXPERT, D), w_hi_map),
            ],
            out_specs=pl.BlockSpec(memory_space=pl.ANY),
            scratch_shapes=[
                pltpu.SMEM((max_tiles * MOE_TM,), jnp.int32),
                pltpu.VMEM((2, MOE_TM, ROW_W), F32),
                pltpu.VMEM((2, MOE_TM, D), F32),
                pltpu.SemaphoreType.DMA((2,)),
                pltpu.SemaphoreType.DMA((2,)),
            ],
        ),
        out_shape=jax.ShapeDtypeStruct((n, D), F32),
        compiler_params=pltpu.CompilerParams(
            dimension_semantics=("arbitrary",), vmem_limit_bytes=VMEM_LIMIT),
        name="moe_sparse",
    )(slot, e_lo, e_hi, n_valid, n_used.reshape(1).astype(jnp.int32),
      rows, ffn_g, final_g, wg, wu, wd, wg, wu, wd)


def _rope_tables(pos):
    half = DK // 2
    freqs = jnp.power(ROPE_BASE, -jnp.arange(half, dtype=F32) / half)
    ang = pos.astype(F32)[:, None] * freqs[None, :]
    return jnp.cos(ang), jnp.sin(ang)


def _decay_tables(L):
    lg = jnp.log(1.0 - jnp.exp2(-5.0 - jnp.arange(N_HEADS, dtype=F32)))
    idx = jnp.arange(L, dtype=F32)
    diff = idx[:, None] - idx[None, :]
    dmat = jnp.where((diff >= 0)[None], jnp.exp(jnp.maximum(diff, 0.0)[None] * lg[:, None, None]), 0.0)
    dq = jnp.exp((idx[:, None] + 1.0) * lg[None, :]).T
    dk = jnp.exp((L - 1.0 - idx)[:, None] * lg[None, :]).T
    gl = jnp.exp(L * lg)
    bl = lambda a: jnp.broadcast_to(a[:, :, None], (N_HEADS, L, LANES))
    return dmat, bl(dq), bl(dk), gl


def kernel(x_prompt, x_sample, state_ret, state_conv, meta_tokens, norm_mix_g, w_in, b_gates, conv_w, conv_b, conv_ln_g, conv_ln_b, w_conv_out, ret_gn_g, w_ret_out, w_o, norm_ffn_g, w_coarse, b_coarse, w_fine, b_fine, w_gate_e, w_up_e, w_down_e, norm_final_g):
    bp, seq, _ = x_prompt.shape
    ns = x_sample.shape[0]
    l = 0
    row = lambda a: a.reshape(1, -1)

    w_in_bf = w_in[l].astype(BF)
    wr = jnp.zeros((D, LANES), F32).at[:, :N_EXPERTS].set(w_fine[l]).at[:, N_EXPERTS:N_EXPERTS + N_GROUPS].set(w_coarse[l])
    wr_hi = wr.astype(BF)
    wr_lo = (wr - wr_hi.astype(F32)).astype(BF)
    br = jnp.zeros((1, LANES), F32).at[0, :N_EXPERTS].set(b_fine[l]).at[0, N_EXPERTS:N_EXPERTS + N_GROUPS].set(b_coarse[l])
    post_args = (row(conv_ln_g[l]), row(conv_ln_b[l]), w_conv_out[l].astype(BF), row(ret_gn_g[l]),
                 w_ret_out[l].astype(BF), row(b_gates[l]), w_o[l].astype(BF), row(norm_ffn_g[l]),
                 wr_hi, wr_lo, br)
    wg = w_gate_e[l].astype(BF)
    wu = w_up_e[l].astype(BF)
    wd = w_down_e[l].astype(BF)
    mix_g = row(norm_mix_g[l])
    cw = conv_w[l]
    cb = row(conv_b[l])

    small = dict(zip(_POST_NAMES, post_args))
    small_args = tuple(small[k] for k in _SMALL_NAMES)
    big_args = (w_in_bf,) + tuple(small[k] for k in _BIG_NAMES[1:])
    r = jnp.arange(PT)
    tri = (r[None, :] < r[:, None]).astype(BF)
    consts = (cw, cb, tri, mix_g)
    decay = _decay_tables(PT)
    x_meta = jnp.concatenate([jnp.zeros((PT - N_META, D), F32), meta_tokens])[None]
    pos_meta = jnp.maximum(jnp.arange(PT, dtype=F32) - (PT - N_META), 0.0)
    _, _, _, s_meta, buf_meta = _prompt_call(
        x_meta, jnp.zeros((N_HEADS, DK, DV), F32), jnp.zeros((HALO, D), F32),
        _rope_tables(pos_meta) + decay, consts, small_args, big_args, "meta_state")
    pos_p = N_META + jnp.arange(seq, dtype=F32)
    rows_p, route_p, counts, s_new_p, buf_p = _prompt_call(
        x_prompt, s_meta[0], buf_meta[0], _rope_tables(pos_p) + decay, consts, small_args, big_args,
        "prompt_mix")
    y_p = _moe_sparse_call(rows_p.reshape(bp * seq, ROW_W), route_p.reshape(bp * seq, LANES),
                           counts[0, :N_BINS].astype(jnp.int32),
                           row(norm_ffn_g[l]), row(norm_final_g), wg, wu, wd).reshape(bp, seq, D)

    xs = x_sample.reshape(ns, D)
    proj_s = _proj_call(xs, mix_g, w_in_bf, tm=ns)
    pos_s = jnp.full((1,), float(PAST_LEN), F32)
    cos_s, sin_s = _rope_tables(pos_s)
    _, _, _, gam = _decay_tables(1)
    o_s, c_s, s_new_s, buf_s = _sample_seq(proj_s, state_ret[l], state_conv[l], cos_s, sin_s, gam, cw, cb)
    x1_s, comb_s = _post_call(c_s, o_s, proj_s, xs, post_args)
    y_s = _moe_call(x1_s, comb_s, row(norm_ffn_g[l]), row(norm_final_g), wg, wu, wd, tm=ns).reshape(ns, 1, D)

    return (y_p, y_s, s_new_p[None], buf_p[:, HALO - (CONV_W - 1):][None], s_new_s[None], buf_s[None])
```

```python
import functools

import jax
import jax.numpy as jnp
import numpy as np
from jax import lax
from jax.experimental import pallas as pl
from jax.experimental.pallas import tpu as pltpu

D = 1024
N_META = 16
PAST_LEN = 16384
CONV_W = 31
N_HEADS = 4
DK = 256
DV = 512
D_QK = N_HEADS * DK
D_V = N_HEADS * DV
ROPE_BASE = 10000.0
N_GROUPS = 4
EPG = 4
N_EXPERTS = 16
D_EXPERT = 512
EPS = 1e-6
D_IN = 2 * D + 2 * D_QK + 2 * D_V + 2 * D

O_GLU_A, O_GLU_B = 0, D
O_Q, O_K = 2 * D, 2 * D + D_QK
O_V = 2 * D + 2 * D_QK
O_GRET = O_V + D_V
O_GA = O_GRET + D_V
O_GB = O_GA + D

N_PAIRS = EPG * (EPG - 1) // 2
N_BINS = N_GROUPS * N_PAIRS

LANES = 128
SUBLANES = 8
MXU_DIM = 256
ROW_W = D + LANES
AUX_W_LO, AUX_W_HI = 0, 1
ROUTE_BIN, ROUTE_RANK = 0, 1
MOE_TM = MXU_DIM
DMA_UNROLL = 8
PT = MXU_DIM
CONV_COLS = MXU_DIM
HALO = 32
VMEM_LIMIT = 56 * 1024 * 1024

BF = jnp.bfloat16
F32 = jnp.float32


def _sigmoid(x):
    return 1.0 / (1.0 + jnp.exp(-x))


def _bdot(a, b):
    return jnp.dot(a.astype(BF), b.astype(BF), preferred_element_type=F32)


def _rmsnorm(x, g):
    return x * lax.rsqrt(jnp.mean(x * x, axis=-1, keepdims=True) + EPS) * g


def _proj_kernel(x_ref, g_ref, w_ref, o_ref):
    h = _rmsnorm(x_ref[...], g_ref[...])
    o_ref[...] = jnp.dot(h.astype(BF), w_ref[...], preferred_element_type=F32)


def _proj_call(x2d, g, w_bf, tm, tn=2048):
    n = x2d.shape[0]
    return pl.pallas_call(
        _proj_kernel,
        grid=(D_IN // tn, n // tm),
        in_specs=[
            pl.BlockSpec((tm, D), lambda j, i: (i, 0)),
            pl.BlockSpec((1, D), lambda j, i: (0, 0)),
            pl.BlockSpec((D, tn), lambda j, i: (0, j)),
        ],
        out_specs=pl.BlockSpec((tm, tn), lambda j, i: (i, j)),
        out_shape=jax.ShapeDtypeStruct((n, D_IN), F32),
        compiler_params=pltpu.CompilerParams(
            dimension_semantics=("arbitrary", "arbitrary"), vmem_limit_bytes=VMEM_LIMIT),
        name="in_proj",
    )(x2d, g, w_bf)


def _route(t, wr_hi_ref, wr_lo_ref, br_ref):
    t_hi = t.astype(BF)
    t_lo = (t - t_hi.astype(F32)).astype(BF)
    wh = wr_hi_ref[...]
    logits = (jnp.dot(t_hi, wh, preferred_element_type=F32)
              + jnp.dot(t_hi, wr_lo_ref[...], preferred_element_type=F32)
              + jnp.dot(t_lo, wh, preferred_element_type=F32)) + br_ref[...]
    lane = lax.broadcasted_iota(jnp.int32, logits.shape, 1)
    neg = jnp.float32(-jnp.inf)
    is_c = (lane >= N_EXPERTS) & (lane < N_EXPERTS + N_GROUPS)
    lc = jnp.where(is_c, logits, neg)
    cmax = jnp.max(lc, axis=-1, keepdims=True)
    csum = jnp.sum(jnp.where(is_c, jnp.exp(lc - cmax), 0.0), axis=-1, keepdims=True)
    p_g = 1.0 / csum
    g_sel = jnp.min(jnp.where(is_c & (lc == cmax), lane - N_EXPERTS, 1 << 20), axis=-1, keepdims=True)
    in_g = (lane < N_EXPERTS) & ((lane >> 2) == g_sel)
    f1 = jnp.where(in_g, logits, neg)
    v1 = jnp.max(f1, axis=-1, keepdims=True)
    i1 = jnp.min(jnp.where(in_g & (f1 == v1), lane, 1 << 20), axis=-1, keepdims=True)
    rest = in_g & (lane != i1)
    f2 = jnp.where(rest, logits, neg)
    v2 = jnp.max(f2, axis=-1, keepdims=True)
    i2 = jnp.min(jnp.where(rest & (f2 == v2), lane, 1 << 20), axis=-1, keepdims=True)
    e2 = jnp.exp(v2 - v1)
    w1 = p_g / (1.0 + e2)
    w2 = e2 * w1
    comb = jnp.where(lane == i1, w1, jnp.where(lane == i2, w2, 0.0))
    first_low = i1 < i2
    a = jnp.where(first_low, i1, i2) & (EPG - 1)
    b = jnp.where(first_low, i2, i1) & (EPG - 1)
    pair = jnp.where(a == 0, 0, jnp.where(a == 1, 3, 5)) + (b - a - 1)
    bin_id = g_sel * N_PAIRS + pair
    return comb, (jnp.where(first_low, w1, w2), jnp.where(first_low, w2, w1), bin_id)


def _silu(x):
    return x * _sigmoid(x)


def _post_tokens(c, o, gret_act, g_a, g_b, x, p):
    mu = jnp.mean(c, axis=-1, keepdims=True)
    dc = c - mu
    var = jnp.mean(dc * dc, axis=-1, keepdims=True)
    cn = dc * lax.rsqrt(var + EPS) * p["ln_g"][...] + p["ln_b"][...]
    y_a = _bdot(_silu(cn), p["w_conv_out"][...])
    parts = []
    for h in range(N_HEADS):
        oh = o[:, h * DV:(h + 1) * DV]
        omu = jnp.mean(oh, axis=-1, keepdims=True)
        od = oh - omu
        ovar = jnp.mean(od * od, axis=-1, keepdims=True)
        parts.append(od * lax.rsqrt(ovar + EPS))
    on = jnp.concatenate(parts, axis=-1) * p["gn_g"][...]
    y_b = _bdot(on * gret_act, p["w_ret_out"][...])
    x1 = x + _bdot(g_a * y_a + g_b * y_b, p["w_o"][...])
    t = _rmsnorm(x1, p["ffn_g"][...])
    comb, pair_route = _route(t, p["wr_hi"], p["wr_lo"], p["br"])
    return x1, comb, pair_route


_POST_NAMES = ("ln_g", "ln_b", "w_conv_out", "gn_g", "w_ret_out", "b_gates", "w_o", "ffn_g",
               "wr_hi", "wr_lo", "br")


def _post_specs(nidx):
    z2 = lambda *_: (0, 0)
    shapes = {"ln_g": (1, D), "ln_b": (1, D), "w_conv_out": (D, D), "gn_g": (1, D_V),
              "w_ret_out": (D_V, D), "b_gates": (1, 2 * D), "w_o": (D, D), "ffn_g": (1, D),
              "wr_hi": (D, LANES), "wr_lo": (D, LANES), "br": (1, LANES)}
    return [pl.BlockSpec(shapes[k], z2) for k in _POST_NAMES]


def _rotary(x, cos, sin):
    x1, x2 = x[:, :LANES], x[:, LANES:]
    return jnp.concatenate([x1 * cos - x2 * sin, x1 * sin + x2 * cos], axis=-1)


_SMALL_NAMES = ("ln_g", "ln_b", "gn_g", "b_gates", "ffn_g", "wr_hi", "wr_lo", "br")
_BIG_NAMES = ("w_in", "w_conv_out", "w_ret_out", "w_o")


def _prompt_kernel(x_ref, buf0_ref, cos_ref, sin_ref, dmat_ref, dq_ref, dk_ref, gl_ref,
                   cw_ref, cb_ref, tri_ref, mixg_ref, *rest):
    ns, nb = len(_SMALL_NAMES), len(_BIG_NAMES)
    small = dict(zip(_SMALL_NAMES, rest[:ns]))
    s0_hbm = rest[ns]
    big_hbm = rest[ns + 1:ns + 1 + nb]
    x1e_ref, route_ref, cnt_out_ref, s_out_hbm, buf_out_ref = rest[ns + 1 + nb:ns + 6 + nb]
    scratch = rest[ns + 6 + nb:]
    big = dict(zip(_BIG_NAMES, scratch[:nb]))
    wsem, ssem, shift_ref, tail_ref, s_ref, c_ref, o_ref, h_ref, cnt_ref = scratch[nb:]
    p = dict(small, **{k: big[k] for k in _BIG_NAMES[1:]})
    w_in = big["w_in"]
    b = pl.program_id(0)
    i = pl.program_id(1)

    @pl.when((b == 0) & (i == 0))
    def _():
        copies = [pltpu.make_async_copy(src, big[k], wsem.at[n])
                  for n, (k, src) in enumerate(zip(_BIG_NAMES, big_hbm))]
        for cp in copies:
            cp.start()
        for cp in copies:
            cp.wait()
        cnt_ref[...] = jnp.zeros_like(cnt_ref)

    @pl.when(i == 0)
    def _():
        load_state = pltpu.make_async_copy(s0_hbm, s_ref, ssem)
        load_state.start()
        load_state.wait()
        tail_ref[...] = buf0_ref[...]

    h_ref[...] = _rmsnorm(x_ref[0], mixg_ref[...]).astype(BF)

    def proj(lo, width):
        return jnp.dot(h_ref[...], w_in[:, lo:lo + width], preferred_element_type=F32)

    base = HALO - (CONV_W - 1)

    def conv_block(cb):
        c0 = cb * CONV_COLS
        par = cb % 2
        u = proj(O_GLU_A + c0, CONV_COLS) * _sigmoid(proj(O_GLU_B + c0, CONV_COLS))
        for half in range(CONV_COLS // LANES):
            cols = slice(c0 + half * LANES, c0 + (half + 1) * LANES)
            uh = u[:, half * LANES:(half + 1) * LANES]
            for s in range(SUBLANES):
                shift_ref[par, s, half, 0:HALO - s, :] = tail_ref[s:HALO, cols]
                shift_ref[par, s, half, HALO - s:HALO - s + PT, :] = uh
            tail_ref[:, cols] = uh[PT - HALO:PT, :]
            acc = jnp.broadcast_to(cb_ref[:, cols], (PT, LANES))
            for j in range(CONV_W):
                s = (base + j) % SUBLANES
                start = base + j - s
                acc = acc + cw_ref[j:j + 1, cols] * shift_ref[par, s, half, start:start + PT, :]
            c_ref[:, cols] = acc

    cos = cos_ref[...]
    sin = sin_ref[...]

    def retention_head(h):
        q = _rotary(proj(O_Q + h * DK, DK), cos, sin)
        k = _rotary(proj(O_K + h * DK, DK), cos, sin) * (DK ** -0.5)
        v = proj(O_V + h * DV, DV).astype(BF)
        dq = dq_ref[h]
        dk = dk_ref[h]
        scores = lax.dot_general(q.astype(BF), k.astype(BF), (((1,), (1,)), ((), ())),
                                 preferred_element_type=F32) * dmat_ref[h]
        s_old = s_ref[h]
        q_dec = jnp.concatenate([q[:, :LANES] * dq, q[:, LANES:] * dq], axis=-1)
        k_dec = jnp.concatenate([k[:, :LANES] * dk, k[:, LANES:] * dk], axis=-1)
        o_ref[:, h * DV:(h + 1) * DV] = (jnp.dot(scores.astype(BF), v, preferred_element_type=F32)
                                         + _bdot(q_dec, s_old))
        s_ref[h] = gl_ref[h] * s_old + lax.dot_general(
            k_dec.astype(BF), v, (((0,), (0,)), ((), ())), preferred_element_type=F32)

    assert D // CONV_COLS == N_HEADS
    for n in range(N_HEADS):
        conv_block(n)
        retention_head(n)

    @pl.when(i == pl.num_programs(1) - 1)
    def _():
        store_state = pltpu.make_async_copy(s_ref, s_out_hbm.at[b], ssem)
        store_state.start()
        store_state.wait()
        buf_out_ref[0] = tail_ref[...]

    bg = small["b_gates"]
    x1, _, (w_lo, w_hi, bin_id) = _post_tokens(c_ref[...], o_ref[...], _silu(proj(O_GRET, D_V)),
                                               _sigmoid(proj(O_GA, D) + bg[:, :D]),
                                               _sigmoid(proj(O_GB, D) + bg[:, D:]), x_ref[0], p)

    lane = lax.broadcasted_iota(jnp.int32, (PT, LANES), 1)
    onehot = lane == bin_id
    earlier = jnp.dot(tri_ref[...], onehot.astype(BF), preferred_element_type=F32)
    rank = jnp.sum(jnp.where(onehot, earlier + cnt_ref[...], 0.0), axis=-1, keepdims=True)
    cnt_ref[...] += jnp.sum(onehot.astype(F32), axis=0, keepdims=True)

    x1e_ref[0, :, 0:D] = x1
    x1e_ref[0, :, D:ROW_W] = jnp.where(lane == AUX_W_LO, w_lo, jnp.where(lane == AUX_W_HI, w_hi, 0.0))
    record = jnp.where(lane == ROUTE_BIN, bin_id, jnp.where(lane == ROUTE_RANK, rank.astype(jnp.int32), 0))
    route_ref[...] = record.T[0:SUBLANES, :]
    cnt_out_ref[...] = cnt_ref[...]


def _prompt_call(x, s0, buf0, tabs, consts, small, big, name):
    b, t, _ = x.shape
    cos, sin, dmat, dq, dk, gl = tabs
    z2 = lambda bi, i: (0, 0)
    z3 = lambda bi, i: (0, 0, 0)
    tok = lambda bi, i: (bi, i, 0)
    whole = lambda a: pl.BlockSpec(a.shape, z2 if a.ndim == 2 else z3)
    any_spec = pl.BlockSpec(memory_space=pl.ANY)
    return pl.pallas_call(
        _prompt_kernel,
        grid=(b, t // PT),
        in_specs=[
            pl.BlockSpec((1, PT, D), tok),
            whole(buf0),
            pl.BlockSpec((PT, LANES), lambda bi, i: (i, 0)),
            pl.BlockSpec((PT, LANES), lambda bi, i: (i, 0)),
            whole(dmat), whole(dq), whole(dk),
            pl.BlockSpec(memory_space=pltpu.SMEM),
        ] + [whole(a) for a in consts] + [whole(a) for a in small] + [any_spec] * (1 + len(big)),
        out_specs=[
            pl.BlockSpec((1, PT, ROW_W), tok),
            pl.BlockSpec((SUBLANES, PT), lambda bi, i: (0, bi * (t // PT) + i)),
            pl.BlockSpec((1, LANES), z2),
            any_spec,
            pl.BlockSpec((1, HALO, D), lambda bi, i: (bi, 0, 0)),
        ],
        out_shape=[
            jax.ShapeDtypeStruct((b, t, ROW_W), F32),
            jax.ShapeDtypeStruct((SUBLANES, b * t), jnp.int32),
            jax.ShapeDtypeStruct((1, LANES), F32),
            jax.ShapeDtypeStruct((b, N_HEADS, DK, DV), F32),
            jax.ShapeDtypeStruct((b, HALO, D), F32),
        ],
        scratch_shapes=[pltpu.VMEM(w.shape, BF) for w in big] + [
            pltpu.SemaphoreType.DMA((len(big),)),
            pltpu.SemaphoreType.DMA,
            pltpu.VMEM((2, SUBLANES, CONV_COLS // LANES, HALO + PT, LANES), F32),
            pltpu.VMEM((HALO, D), F32),
            pltpu.VMEM((N_HEADS, DK, DV), F32),
            pltpu.VMEM((PT, D), F32),
            pltpu.VMEM((PT, D_V), F32),
            pltpu.VMEM((PT, D), BF),
            pltpu.VMEM((1, LANES), F32),
        ],
        compiler_params=pltpu.CompilerParams(
            dimension_semantics=("arbitrary", "arbitrary"), vmem_limit_bytes=VMEM_LIMIT),
        name=name,
    )(x, buf0, cos, sin, dmat, dq, dk, gl, *consts, *small, s0, *big)


SEQ_BLK = 2
MXU_ROWS = 16


def _sample_seq_step(step, proj_ref, s_ref, sc_ref, cos_ref, sin_ref, gam_ref, cw_ref, cb_ref,
                     o_ref, c_ref, s_out_ref, sc_out_ref):
    cos = cos_ref[...]
    sin = sin_ref[...]
    row = lax.broadcasted_iota(jnp.int32, (MXU_ROWS, DK), 0)
    for s in range(SEQ_BLK):
        seq = pl.ds(step * SEQ_BLK + s, 1)
        us = proj_ref[seq, O_GLU_A:O_GLU_A + D] * _sigmoid(proj_ref[seq, O_GLU_B:O_GLU_B + D])
        c_ref[seq, :] = (
            jnp.sum(cw_ref[0:CONV_W - 1, :] * sc_ref[s], axis=0, keepdims=True)
            + cw_ref[CONV_W - 1:CONV_W, :] * us + cb_ref[...])
        sc_out_ref[s, 0:CONV_W - 2, :] = sc_ref[s, 1:CONV_W - 1, :]
        sc_out_ref[s, CONV_W - 2:CONV_W - 1, :] = us
        for h in range(N_HEADS):
            q = _rotary(proj_ref[seq, O_Q + h * DK:O_Q + (h + 1) * DK], cos, sin)
            k = _rotary(proj_ref[seq, O_K + h * DK:O_K + (h + 1) * DK], cos, sin) * (DK ** -0.5)
            v = proj_ref[seq, O_V + h * DV:O_V + (h + 1) * DV]
            gam = gam_ref[h]
            qk = jnp.sum(q * k, axis=-1, keepdims=True)
            s_old = s_ref[s, h]
            o8 = _bdot(jnp.broadcast_to(q * gam, (MXU_ROWS, DK)), s_old)
            o_ref[seq, h * DV:(h + 1) * DV] = qk * v + o8[0:1, :]
            k8 = jnp.where(row == 0, jnp.broadcast_to(k, (MXU_ROWS, DK)), 0.0).astype(BF)
            v8 = jnp.broadcast_to(v, (MXU_ROWS, DV)).astype(BF)
            kv = lax.dot_general(k8, v8, (((0,), (0,)), ((), ())), preferred_element_type=F32)
            s_out_ref[s, h] = gam * s_old + kv


def _post_kernel(c_ref, o_ref, gret_ref, gates_ref, x_ref, *rest):
    p = dict(zip(_POST_NAMES, rest[:len(_POST_NAMES)]))
    x1_ref, comb_ref = rest[len(_POST_NAMES):]
    bg = p["b_gates"][...]
    x1, comb, _ = _post_tokens(c_ref[...], o_ref[...], _silu(gret_ref[...]),
                               _sigmoid(gates_ref[:, :D] + bg[:, :D]), _sigmoid(gates_ref[:, D:] + bg[:, D:]),
                               x_ref[...], p)
    x1_ref[...] = x1
    comb_ref[...] = comb


def _post_call(c, o, proj_s, x_s, post_args):
    n = x_s.shape[0]
    z2 = lambda i: (0, 0)
    return pl.pallas_call(
        _post_kernel,
        grid=(1,),
        in_specs=[
            pl.BlockSpec((n, D), z2),
            pl.BlockSpec((n, D_V), z2),
            pl.BlockSpec((n, D_V), lambda i: (0, O_GRET // D_V)),
            pl.BlockSpec((n, 2 * D), lambda i: (0, O_GA // (2 * D))),
            pl.BlockSpec((n, D), z2),
        ] + _post_specs(1),
        out_specs=[pl.BlockSpec((n, D), z2), pl.BlockSpec((n, LANES), z2)],
        out_shape=[jax.ShapeDtypeStruct((n, D), F32), jax.ShapeDtypeStruct((n, LANES), F32)],
        compiler_params=pltpu.CompilerParams(
            dimension_semantics=("arbitrary",), vmem_limit_bytes=VMEM_LIMIT),
        name="sample_post",
    )(c, o, proj_s, proj_s, x_s, *post_args)


def _moe_kernel(x1_ref, comb_ref, fg_ref, ng_ref, wg_ref, wu_ref, wd_ref, y_ref, t_ref, acc_ref):
    e = pl.program_id(1)

    @pl.when(e == 0)
    def _():
        t_ref[...] = _rmsnorm(x1_ref[...], fg_ref[...]).astype(BF)
        acc_ref[...] = jnp.zeros_like(acc_ref)

    t = t_ref[...]
    hg = jnp.dot(t, wg_ref[0], preferred_element_type=F32)
    hu = jnp.dot(t, wu_ref[0], preferred_element_type=F32)
    comb = comb_ref[...]
    lane = lax.broadcasted_iota(jnp.int32, comb.shape, 1)
    ce = jnp.sum(jnp.where(lane == e, comb, 0.0), axis=-1, keepdims=True)
    act = (hg * _sigmoid(hg)) * hu * ce
    acc_ref[...] += jnp.dot(act.astype(BF), wd_ref[0], preferred_element_type=F32)

    @pl.when(e == N_EXPERTS - 1)
    def _():
        y_ref[...] = _rmsnorm(x1_ref[...] + acc_ref[...], ng_ref[...])


def _moe_call(x1, comb, ffn_g, final_g, wg, wu, wd, tm):
    n = x1.shape[0]
    z2 = lambda i, e: (0, 0)
    return pl.pallas_call(
        _moe_kernel,
        grid=(n // tm, N_EXPERTS),
        in_specs=[
            pl.BlockSpec((tm, D), lambda i, e: (i, 0)),
            pl.BlockSpec((tm, LANES), lambda i, e: (i, 0)),
            pl.BlockSpec((1, D), z2),
            pl.BlockSpec((1, D), z2),
            pl.BlockSpec((1, D, D_EXPERT), lambda i, e: (e, 0, 0)),
            pl.BlockSpec((1, D, D_EXPERT), lambda i, e: (e, 0, 0)),
            pl.BlockSpec((1, D_EXPERT, D), lambda i, e: (e, 0, 0)),
        ],
        out_specs=pl.BlockSpec((tm, D), lambda i, e: (i, 0)),
        out_shape=jax.ShapeDtypeStruct((n, D), F32),
        scratch_shapes=[pltpu.VMEM((tm, D), BF), pltpu.VMEM((tm, D), F32)],
        compiler_params=pltpu.CompilerParams(
            dimension_semantics=("arbitrary", "arbitrary"), vmem_limit_bytes=VMEM_LIMIT),
        name="moe",
    )(x1, comb, ffn_g, final_g, wg, wu, wd)


def _moe_sparse_kernel(n_seq_steps, slot_ref, e_lo_ref, e_hi_ref, nvalid_ref, nused_ref,
                       rows_hbm, fg_ref, ng_ref, wg_lo, wu_lo, wd_lo, wg_hi, wu_hi, wd_hi,
                       dproj_ref, ds_ref, dsc_ref, dcos_ref, dsin_ref, dgam_ref, dcw_ref, dcb_ref,
                       y_hbm, do_ref, dc_ref, ds_out_ref, dsc_out_ref,
                       inv_ref, xbuf, ybuf, gsem, ssem):
    t = pl.program_id(0)
    n_used = nused_ref[0]
    n_tok = slot_ref.shape[0]
    cur = lax.rem(t, 2)

    @pl.when(t < n_seq_steps)
    def _():
        _sample_seq_step(t, dproj_ref, ds_ref, dsc_ref, dcos_ref, dsin_ref, dgam_ref, dcw_ref, dcb_ref,
                         do_ref, dc_ref, ds_out_ref, dsc_out_ref)

    def gather_copy(r, tok, buf):
        return pltpu.make_async_copy(rows_hbm.at[pl.ds(tok, 1), :], xbuf.at[buf, pl.ds(r, 1), :], gsem.at[buf])

    def scatter_copy(r, tok, buf):
        return pltpu.make_async_copy(ybuf.at[buf, pl.ds(r, 1), :], y_hbm.at[pl.ds(tok, 1), :], ssem.at[buf])

    def start_rows(copy, tile, buf):
        base = tile * MOE_TM
        n_rows = nvalid_ref[tile]
        n_blk = n_rows // DMA_UNROLL

        def blk(b, c):
            for u in range(DMA_UNROLL):
                r = b * DMA_UNROLL + u
                copy(r, inv_ref[base + r], buf).start()
            return c
        lax.fori_loop(0, n_blk, blk, 0)

        def tail(r, c):
            copy(r, inv_ref[base + r], buf).start()
            return c
        lax.fori_loop(n_blk * DMA_UNROLL, n_rows, tail, 0)

    def wait_rows(copy, tile, buf):
        n_rows = nvalid_ref[tile]
        k = MOE_TM
        while k >= 1:
            @pl.when((n_rows & k) != 0)
            def _(k=k):
                src = rows_hbm if copy is gather_copy else ybuf.at[buf]
                dst = xbuf.at[buf] if copy is gather_copy else y_hbm
                pltpu.make_async_copy(src.at[pl.ds(0, k), :], dst.at[pl.ds(0, k), :],
                                      (gsem if copy is gather_copy else ssem).at[buf]).wait()
            k //= 2

    @pl.when(t == 0)
    def _():
        def fill(tok, c):
            inv_ref[slot_ref[tok]] = tok
            return c
        lax.fori_loop(0, n_tok, fill, 0, unroll=DMA_UNROLL)
        xbuf[...] = jnp.zeros_like(xbuf)
        start_rows(gather_copy, 0, 0)

    @pl.when(t < n_used)
    def _():
        wait_rows(gather_copy, t, cur)

        @pl.when(t + 1 < n_used)
        def _():
            start_rows(gather_copy, t + 1, 1 - cur)

        x1 = xbuf[cur, :, 0:D]
        aux = xbuf[cur, :, D:ROW_W]
        tb = _rmsnorm(x1, fg_ref[...]).astype(BF)
        acc = None
        for wg, wu, wd, lane_w in ((wg_lo, wu_lo, wd_lo, AUX_W_LO), (wg_hi, wu_hi, wd_hi, AUX_W_HI)):
            hg = jnp.dot(tb, wg[0], preferred_element_type=F32)
            hu = jnp.dot(tb, wu[0], preferred_element_type=F32)
            act = (hg * _sigmoid(hg)) * hu * aux[:, lane_w:lane_w + 1]
            part = jnp.dot(act.astype(BF), wd[0], preferred_element_type=F32)
            acc = part if acc is None else acc + part
        y = _rmsnorm(x1 + acc, ng_ref[...])

        @pl.when(t >= 2)
        def _():
            wait_rows(scatter_copy, jnp.maximum(t - 2, 0), cur)
        ybuf[cur] = y
        start_rows(scatter_copy, t, cur)

        @pl.when(t == n_used - 1)
        def _():
            wait_rows(scatter_copy, t, cur)

            @pl.when(t >= 1)
            def _():
                wait_rows(scatter_copy, jnp.maximum(t - 1, 0), 1 - cur)


def _bin_tables():
    lo, hi = [], []
    for g in range(N_GROUPS):
        for a in range(EPG):
            for b in range(a + 1, EPG):
                lo.append(g * EPG + a)
                hi.append(g * EPG + b)
    return jnp.array(lo, jnp.int32), jnp.array(hi, jnp.int32)


def _moe_sparse_call(rows, route, counts, ffn_g, final_g, wg, wu, wd, decode):
    proj_s, state_ret, state_conv, cos_s, sin_s, gam, cw, cb = decode
    n_seq = proj_s.shape[0]
    n_seq_steps = n_seq // SEQ_BLK
    n = route.shape[1]
    max_tiles = n // MOE_TM + N_BINS
    assert max_tiles >= n_seq_steps
    padded = ((counts + MOE_TM - 1) // MOE_TM) * MOE_TM
    ends = jnp.cumsum(padded)
    offs = ends - padded
    n_used = ends[-1] // MOE_TM
    slot = offs[route[ROUTE_BIN]] + route[ROUTE_RANK]
    tile_start = jnp.minimum(jnp.arange(max_tiles, dtype=jnp.int32), n_used - 1) * MOE_TM
    tile_bin = jnp.sum((tile_start[:, None] >= ends[None, :]).astype(jnp.int32), axis=1)
    bin_lo, bin_hi = _bin_tables()
    e_lo = bin_lo[tile_bin]
    e_hi = bin_hi[tile_bin]
    tile_id = jnp.arange(max_tiles, dtype=jnp.int32)
    n_valid = jnp.where(tile_id < n_used,
                        jnp.clip(counts[tile_bin] - (tile_start - offs[tile_bin]), 0, MOE_TM), 0)

    z2 = lambda t, *_: (0, 0)
    w_lo_map = lambda t, slot_r, lo_r, hi_r, nv_r, nu_r: (lo_r[t], 0, 0)
    w_hi_map = lambda t, slot_r, lo_r, hi_r, nv_r, nu_r: (hi_r[t], 0, 0)
    seq3 = lambda t, *_: (jnp.minimum(t, n_seq_steps - 1), 0, 0)
    seq4 = lambda t, *_: (jnp.minimum(t, n_seq_steps - 1), 0, 0, 0)
    y, o, c, s_new, sc_new = pl.pallas_call(
        functools.partial(_moe_sparse_kernel, n_seq_steps),
        grid_spec=pltpu.PrefetchScalarGridSpec(
            num_scalar_prefetch=5,
            grid=(max_tiles,),
            in_specs=[
                pl.BlockSpec(memory_space=pl.ANY),
                pl.BlockSpec((1, D), z2),
                pl.BlockSpec((1, D), z2),
                pl.BlockSpec((1, D, D_EXPERT), w_lo_map),
                pl.BlockSpec((1, D, D_EXPERT), w_lo_map),
                pl.BlockSpec((1, D_EXPERT, D), w_lo_map),
                pl.BlockSpec((1, D, D_EXPERT), w_hi_map),
                pl.BlockSpec((1, D, D_EXPERT), w_hi_map),
                pl.BlockSpec((1, D_EXPERT, D), w_hi_map),
                pl.BlockSpec((n_seq, D_IN), z2),
                pl.BlockSpec((SEQ_BLK, N_HEADS, DK, DV), seq4),
                pl.BlockSpec((SEQ_BLK, CONV_W - 1, D), seq3),
                pl.BlockSpec((1, LANES), z2),
                pl.BlockSpec((1, LANES), z2),
                pl.BlockSpec(memory_space=pltpu.SMEM),
                pl.BlockSpec((CONV_W, D), z2),
                pl.BlockSpec((1, D), z2),
            ],
            out_specs=[
                pl.BlockSpec(memory_space=pl.ANY),
                pl.BlockSpec((n_seq, D_V), z2),
                pl.BlockSpec((n_seq, D), z2),
                pl.BlockSpec((SEQ_BLK, N_HEADS, DK, DV), seq4),
                pl.BlockSpec((SEQ_BLK, CONV_W - 1, D), seq3),
            ],
            scratch_shapes=[
                pltpu.SMEM((max_tiles * MOE_TM,), jnp.int32),
                pltpu.VMEM((2, MOE_TM, ROW_W), F32),
                pltpu.VMEM((2, MOE_TM, D), F32),
                pltpu.SemaphoreType.DMA((2,)),
                pltpu.SemaphoreType.DMA((2,)),
            ],
        ),
        out_shape=[
            jax.ShapeDtypeStruct((n, D), F32),
            jax.ShapeDtypeStruct((n_seq, D_V), F32),
            jax.ShapeDtypeStruct((n_seq, D), F32),
            jax.ShapeDtypeStruct(state_ret.shape, F32),
            jax.ShapeDtypeStruct(state_conv.shape, F32),
        ],
        compiler_params=pltpu.CompilerParams(
            dimension_semantics=("arbitrary",), vmem_limit_bytes=VMEM_LIMIT),
        name="moe_sparse",
    )(slot, e_lo, e_hi, n_valid, n_used.reshape(1).astype(jnp.int32),
      rows, ffn_g, final_g, wg, wu, wd, wg, wu, wd,
      proj_s, state_ret, state_conv, cos_s, sin_s, gam, cw, cb)
    return y, o, c, s_new, sc_new


def _rope_tables(pos):
    half = DK // 2
    freqs = np.power(np.float64(ROPE_BASE), -np.arange(half, dtype=np.float64) / half)
    ang = pos.astype(np.float64)[:, None] * freqs[None, :]
    return jnp.asarray(np.cos(ang), F32), jnp.asarray(np.sin(ang), F32)


def _decay_tables(L):
    lg = np.log(1.0 - np.exp2(-5.0 - np.arange(N_HEADS, dtype=np.float64)))
    idx = np.arange(L, dtype=np.float64)
    diff = idx[:, None] - idx[None, :]
    dmat = np.where((diff >= 0)[None], np.exp(np.maximum(diff, 0.0)[None] * lg[:, None, None]), 0.0)
    dq = np.exp((idx[:, None] + 1.0) * lg[None, :]).T
    dk = np.exp((L - 1.0 - idx)[:, None] * lg[None, :]).T
    gl = np.exp(L * lg)
    bl = lambda a: jnp.asarray(np.broadcast_to(a[:, :, None], (N_HEADS, L, LANES)), F32)
    return jnp.asarray(dmat, F32), bl(dq), bl(dk), jnp.asarray(gl, F32)


def kernel(x_prompt, x_sample, state_ret, state_conv, meta_tokens, norm_mix_g, w_in, b_gates, conv_w, conv_b, conv_ln_g, conv_ln_b, w_conv_out, ret_gn_g, w_ret_out, w_o, norm_ffn_g, w_coarse, b_coarse, w_fine, b_fine, w_gate_e, w_up_e, w_down_e, norm_final_g):
    bp, seq, _ = x_prompt.shape
    ns = x_sample.shape[0]
    l = 0
    row = lambda a: a.reshape(1, -1)

    w_in_bf = w_in[l].astype(BF)
    wr = jnp.zeros((D, LANES), F32).at[:, :N_EXPERTS].set(w_fine[l]).at[:, N_EXPERTS:N_EXPERTS + N_GROUPS].set(w_coarse[l])
    wr_hi = wr.astype(BF)
    wr_lo = (wr - wr_hi.astype(F32)).astype(BF)
    br = jnp.zeros((1, LANES), F32).at[0, :N_EXPERTS].set(b_fine[l]).at[0, N_EXPERTS:N_EXPERTS + N_GROUPS].set(b_coarse[l])
    post_args = (row(conv_ln_g[l]), row(conv_ln_b[l]), w_conv_out[l].astype(BF), row(ret_gn_g[l]),
                 w_ret_out[l].astype(BF), row(b_gates[l]), w_o[l].astype(BF), row(norm_ffn_g[l]),
                 wr_hi, wr_lo, br)
    wg = w_gate_e[l].astype(BF)
    wu = w_up_e[l].astype(BF)
    wd = w_down_e[l].astype(BF)
    mix_g = row(norm_mix_g[l])
    cw = conv_w[l]
    cb = row(conv_b[l])

    small = dict(zip(_POST_NAMES, post_args))
    small_args = tuple(small[k] for k in _SMALL_NAMES)
    big_args = (w_in_bf,) + tuple(small[k] for k in _BIG_NAMES[1:])
    r = np.arange(PT)
    tri = jnp.asarray(r[None, :] < r[:, None], BF)
    consts = (cw, cb, tri, mix_g)
    decay = _decay_tables(PT)
    x_meta = jnp.concatenate([jnp.zeros((PT - N_META, D), F32), meta_tokens])[None]
    pos_meta = np.maximum(np.arange(PT) - (PT - N_META), 0)
    _, _, _, s_meta, buf_meta = _prompt_call(
        x_meta, jnp.zeros((N_HEADS, DK, DV), F32), jnp.zeros((HALO, D), F32),
        _rope_tables(pos_meta) + decay, consts, small_args, big_args, "meta_state")
    pos_p = N_META + np.arange(seq)
    rows_p, route_p, counts, s_new_p, buf_p = _prompt_call(
        x_prompt, s_meta[0], buf_meta[0], _rope_tables(pos_p) + decay, consts, small_args, big_args,
        "prompt_mix")

    xs = x_sample.reshape(ns, D)
    proj_s = _proj_call(xs, mix_g, w_in_bf, tm=ns)
    pos_s = np.full((1,), PAST_LEN)
    cos_s, sin_s = _rope_tables(pos_s)
    _, _, _, gam = _decay_tables(1)

    y_p, o_s, c_s, s_new_s, buf_s = _moe_sparse_call(
        rows_p.reshape(bp * seq, ROW_W), route_p,
        counts[0, :N_BINS].astype(jnp.int32), row(norm_ffn_g[l]), row(norm_final_g), wg, wu, wd,
        (proj_s, state_ret[l], state_conv[l], cos_s, sin_s, gam, cw, cb))

    x1_s, comb_s = _post_call(c_s, o_s, proj_s, xs, post_args)
    y_s = _moe_call(x1_s, comb_s, row(norm_ffn_g[l]), row(norm_final_g), wg, wu, wd, tm=ns).reshape(ns, 1, D)

    return (y_p.reshape(bp, seq, D), y_s, s_new_p[None], buf_p[:, HALO - (CONV_W - 1):][None],
            s_new_s[None], buf_s[None])
```

```python
import functools

import jax
import jax.numpy as jnp
import numpy as np
from jax import lax
from jax.experimental import pallas as pl
from jax.experimental.pallas import tpu as pltpu

D = 1024
N_META = 16
PAST_LEN = 16384
CONV_W = 31
N_HEADS = 4
DK = 256
DV = 512
D_QK = N_HEADS * DK
D_V = N_HEADS * DV
ROPE_BASE = 10000.0
N_GROUPS = 4
EPG = 4
N_EXPERTS = 16
D_EXPERT = 512
EPS = 1e-6
D_IN = 2 * D + 2 * D_QK + 2 * D_V + 2 * D

O_GLU_A, O_GLU_B = 0, D
O_Q, O_K = 2 * D, 2 * D + D_QK
O_V = 2 * D + 2 * D_QK
O_GRET = O_V + D_V
O_GA = O_GRET + D_V
O_GB = O_GA + D

N_PAIRS = EPG * (EPG - 1) // 2
N_BINS = N_GROUPS * N_PAIRS

LANES = 128
SUBLANES = 8
MXU_DIM = 256
ROW_W = D + LANES
AUX_W_LO, AUX_W_HI = 0, 1
ROUTE_BIN, ROUTE_RANK = 0, 1
MOE_TM = MXU_DIM
DMA_UNROLL = 8
PT = MXU_DIM
CONV_COLS = MXU_DIM
HALO = 32
VMEM_LIMIT = 56 * 1024 * 1024

BF = jnp.bfloat16
F32 = jnp.float32


def _sigmoid(x):
    return 1.0 / (1.0 + jnp.exp(-x))


def _bdot(a, b):
    return jnp.dot(a.astype(BF), b.astype(BF), preferred_element_type=F32)


def _rmsnorm(x, g):
    return x * lax.rsqrt(jnp.mean(x * x, axis=-1, keepdims=True) + EPS) * g


def _proj_kernel(x_ref, g_ref, w_ref, o_ref):
    h = _rmsnorm(x_ref[...], g_ref[...])
    o_ref[...] = jnp.dot(h.astype(BF), w_ref[...], preferred_element_type=F32)


def _proj_call(x2d, g, w_bf, tm, tn=2048):
    n = x2d.shape[0]
    return pl.pallas_call(
        _proj_kernel,
        grid=(D_IN // tn, n // tm),
        in_specs=[
            pl.BlockSpec((tm, D), lambda j, i: (i, 0)),
            pl.BlockSpec((1, D), lambda j, i: (0, 0)),
            pl.BlockSpec((D, tn), lambda j, i: (0, j)),
        ],
        out_specs=pl.BlockSpec((tm, tn), lambda j, i: (i, j)),
        out_shape=jax.ShapeDtypeStruct((n, D_IN), F32),
        compiler_params=pltpu.CompilerParams(
            dimension_semantics=("arbitrary", "arbitrary"), vmem_limit_bytes=VMEM_LIMIT),
        name="in_proj",
    )(x2d, g, w_bf)


def _route(t, wr_hi_ref, wr_lo_ref, br_ref):
    t_hi = t.astype(BF)
    t_lo = (t - t_hi.astype(F32)).astype(BF)
    wh = wr_hi_ref[...]
    logits = (jnp.dot(t_hi, wh, preferred_element_type=F32)
              + jnp.dot(t_hi, wr_lo_ref[...], preferred_element_type=F32)
              + jnp.dot(t_lo, wh, preferred_element_type=F32)) + br_ref[...]
    lane = lax.broadcasted_iota(jnp.int32, logits.shape, 1)
    neg = jnp.float32(-jnp.inf)
    is_c = (lane >= N_EXPERTS) & (lane < N_EXPERTS + N_GROUPS)
    lc = jnp.where(is_c, logits, neg)
    cmax = jnp.max(lc, axis=-1, keepdims=True)
    csum = jnp.sum(jnp.where(is_c, jnp.exp(lc - cmax), 0.0), axis=-1, keepdims=True)
    p_g = 1.0 / csum
    g_sel = jnp.min(jnp.where(is_c & (lc == cmax), lane - N_EXPERTS, 1 << 20), axis=-1, keepdims=True)
    in_g = (lane < N_EXPERTS) & ((lane >> 2) == g_sel)
    f1 = jnp.where(in_g, logits, neg)
    v1 = jnp.max(f1, axis=-1, keepdims=True)
    i1 = jnp.min(jnp.where(in_g & (f1 == v1), lane, 1 << 20), axis=-1, keepdims=True)
    rest = in_g & (lane != i1)
    f2 = jnp.where(rest, logits, neg)
    v2 = jnp.max(f2, axis=-1, keepdims=True)
    i2 = jnp.min(jnp.where(rest & (f2 == v2), lane, 1 << 20), axis=-1, keepdims=True)
    e2 = jnp.exp(v2 - v1)
    w1 = p_g / (1.0 + e2)
    w2 = e2 * w1
    comb = jnp.where(lane == i1, w1, jnp.where(lane == i2, w2, 0.0))
    first_low = i1 < i2
    a = jnp.where(first_low, i1, i2) & (EPG - 1)
    b = jnp.where(first_low, i2, i1) & (EPG - 1)
    pair = jnp.where(a == 0, 0, jnp.where(a == 1, 3, 5)) + (b - a - 1)
    bin_id = g_sel * N_PAIRS + pair
    return comb, (jnp.where(first_low, w1, w2), jnp.where(first_low, w2, w1), bin_id)


def _silu(x):
    return x * _sigmoid(x)


def _post_tokens(c, o, gret_act, g_a, g_b, x, p):
    mu = jnp.mean(c, axis=-1, keepdims=True)
    dc = c - mu
    var = jnp.mean(dc * dc, axis=-1, keepdims=True)
    cn = dc * lax.rsqrt(var + EPS) * p["ln_g"][...] + p["ln_b"][...]
    y_a = _bdot(_silu(cn), p["w_conv_out"][...])
    parts = []
    for h in range(N_HEADS):
        oh = o[:, h * DV:(h + 1) * DV]
        omu = jnp.mean(oh, axis=-1, keepdims=True)
        od = oh - omu
        ovar = jnp.mean(od * od, axis=-1, keepdims=True)
        parts.append(od * lax.rsqrt(ovar + EPS))
    on = jnp.concatenate(parts, axis=-1) * p["gn_g"][...]
    y_b = _bdot(on * gret_act, p["w_ret_out"][...])
    x1 = x + _bdot(g_a * y_a + g_b * y_b, p["w_o"][...])
    t = _rmsnorm(x1, p["ffn_g"][...])
    comb, pair_route = _route(t, p["wr_hi"], p["wr_lo"], p["br"])
    return x1, comb, pair_route


_POST_NAMES = ("ln_g", "ln_b", "w_conv_out", "gn_g", "w_ret_out", "b_gates", "w_o", "ffn_g",
               "wr_hi", "wr_lo", "br")


def _post_specs(nidx):
    z2 = lambda *_: (0, 0)
    shapes = {"ln_g": (1, D), "ln_b": (1, D), "w_conv_out": (D, D), "gn_g": (1, D_V),
              "w_ret_out": (D_V, D), "b_gates": (1, 2 * D), "w_o": (D, D), "ffn_g": (1, D),
              "wr_hi": (D, LANES), "wr_lo": (D, LANES), "br": (1, LANES)}
    return [pl.BlockSpec(shapes[k], z2) for k in _POST_NAMES]


def _rotary(x, cos, sin):
    x1, x2 = x[:, :LANES], x[:, LANES:]
    return jnp.concatenate([x1 * cos - x2 * sin, x1 * sin + x2 * cos], axis=-1)


_SMALL_NAMES = ("ln_g", "ln_b", "gn_g", "b_gates", "ffn_g", "wr_hi", "wr_lo", "br")
_BIG_NAMES = ("w_in", "w_conv_out", "w_ret_out", "w_o")


def _prompt_kernel(x_ref, buf0_ref, cos_ref, sin_ref, dmat_ref, dq_ref, dk_ref, gl_ref,
                   cw_ref, cb_ref, tri_ref, mixg_ref, *rest):
    ns, nb = len(_SMALL_NAMES), len(_BIG_NAMES)
    small = dict(zip(_SMALL_NAMES, rest[:ns]))
    s0_hbm = rest[ns]
    big_hbm = rest[ns + 1:ns + 1 + nb]
    x1e_ref, route_ref, cnt_out_ref, s_out_hbm, buf_out_ref = rest[ns + 1 + nb:ns + 6 + nb]
    scratch = rest[ns + 6 + nb:]
    big = dict(zip(_BIG_NAMES, scratch[:nb]))
    wsem, ssem, shift_ref, tail_ref, s_ref, c_ref, o_ref, h_ref, cnt_ref = scratch[nb:]
    p = dict(small, **{k: big[k] for k in _BIG_NAMES[1:]})
    w_in = big["w_in"]
    b = pl.program_id(0)
    i = pl.program_id(1)

    @pl.when((b == 0) & (i == 0))
    def _():
        copies = [pltpu.make_async_copy(src, big[k], wsem.at[n])
                  for n, (k, src) in enumerate(zip(_BIG_NAMES, big_hbm))]
        for cp in copies:
            cp.start()
        for cp in copies:
            cp.wait()
        cnt_ref[...] = jnp.zeros_like(cnt_ref)

    @pl.when(i == 0)
    def _():
        load_state = pltpu.make_async_copy(s0_hbm, s_ref, ssem)
        load_state.start()
        load_state.wait()
        tail_ref[...] = buf0_ref[...]

    h_ref[...] = _rmsnorm(x_ref[0], mixg_ref[...]).astype(BF)

    def proj(lo, width):
        return jnp.dot(h_ref[...], w_in[:, lo:lo + width], preferred_element_type=F32)

    base = HALO - (CONV_W - 1)

    def conv_block(cb):
        c0 = cb * CONV_COLS
        par = cb % 2
        u = proj(O_GLU_A + c0, CONV_COLS) * _sigmoid(proj(O_GLU_B + c0, CONV_COLS))
        for half in range(CONV_COLS // LANES):
            cols = slice(c0 + half * LANES, c0 + (half + 1) * LANES)
            uh = u[:, half * LANES:(half + 1) * LANES]
            for s in range(SUBLANES):
                shift_ref[par, s, half, 0:HALO - s, :] = tail_ref[s:HALO, cols]
                shift_ref[par, s, half, HALO - s:HALO - s + PT, :] = uh
            tail_ref[:, cols] = uh[PT - HALO:PT, :]
            acc = jnp.broadcast_to(cb_ref[:, cols], (PT, LANES))
            for j in range(CONV_W):
                s = (base + j) % SUBLANES
                start = base + j - s
                acc = acc + cw_ref[j:j + 1, cols] * shift_ref[par, s, half, start:start + PT, :]
            c_ref[:, cols] = acc

    cos = cos_ref[...]
    sin = sin_ref[...]

    def retention_head(h):
        q = _rotary(proj(O_Q + h * DK, DK), cos, sin)
        k = _rotary(proj(O_K + h * DK, DK), cos, sin) * (DK ** -0.5)
        v = proj(O_V + h * DV, DV).astype(BF)
        dq = dq_ref[h]
        dk = dk_ref[h]
        scores = lax.dot_general(q.astype(BF), k.astype(BF), (((1,), (1,)), ((), ())),
                                 preferred_element_type=F32) * dmat_ref[h]
        s_old = s_ref[h]
        q_dec = jnp.concatenate([q[:, :LANES] * dq, q[:, LANES:] * dq], axis=-1)
        k_dec = jnp.concatenate([k[:, :LANES] * dk, k[:, LANES:] * dk], axis=-1)
        o_ref[:, h * DV:(h + 1) * DV] = (jnp.dot(scores.astype(BF), v, preferred_element_type=F32)
                                         + _bdot(q_dec, s_old))
        s_ref[h] = gl_ref[h] * s_old + lax.dot_general(
            k_dec.astype(BF), v, (((0,), (0,)), ((), ())), preferred_element_type=F32)

    assert D // CONV_COLS == N_HEADS
    for n in range(N_HEADS):
        conv_block(n)
        retention_head(n)

    @pl.when(i == pl.num_programs(1) - 1)
    def _():
        store_state = pltpu.make_async_copy(s_ref, s_out_hbm.at[b], ssem)
        store_state.start()
        store_state.wait()
        buf_out_ref[0] = tail_ref[...]

    bg = small["b_gates"]
    x1, _, (w_lo, w_hi, bin_id) = _post_tokens(c_ref[...], o_ref[...], _silu(proj(O_GRET, D_V)),
                                               _sigmoid(proj(O_GA, D) + bg[:, :D]),
                                               _sigmoid(proj(O_GB, D) + bg[:, D:]), x_ref[0], p)

    lane = lax.broadcasted_iota(jnp.int32, (PT, LANES), 1)
    onehot = lane == bin_id
    earlier = jnp.dot(tri_ref[...], onehot.astype(BF), preferred_element_type=F32)
    rank = jnp.sum(jnp.where(onehot, earlier + cnt_ref[...], 0.0), axis=-1, keepdims=True)
    cnt_ref[...] += jnp.sum(onehot.astype(F32), axis=0, keepdims=True)

    x1e_ref[0, :, 0:D] = x1
    x1e_ref[0, :, D:ROW_W] = jnp.where(lane == AUX_W_LO, w_lo, jnp.where(lane == AUX_W_HI, w_hi, 0.0))
    record = jnp.where(lane == ROUTE_BIN, bin_id, jnp.where(lane == ROUTE_RANK, rank.astype(jnp.int32), 0))
    route_ref[...] = record.T[0:SUBLANES, :]
    cnt_out_ref[...] = cnt_ref[...]


def _prompt_call(x, s0, buf0, tabs, consts, small, big, name):
    b, t, _ = x.shape
    cos, sin, dmat, dq, dk, gl = tabs
    z2 = lambda bi, i: (0, 0)
    z3 = lambda bi, i: (0, 0, 0)
    tok = lambda bi, i: (bi, i, 0)
    whole = lambda a: pl.BlockSpec(a.shape, z2 if a.ndim == 2 else z3)
    any_spec = pl.BlockSpec(memory_space=pl.ANY)
    return pl.pallas_call(
        _prompt_kernel,
        grid=(b, t // PT),
        in_specs=[
            pl.BlockSpec((1, PT, D), tok),
            whole(buf0),
            pl.BlockSpec((PT, LANES), lambda bi, i: (i, 0)),
            pl.BlockSpec((PT, LANES), lambda bi, i: (i, 0)),
            whole(dmat), whole(dq), whole(dk),
            pl.BlockSpec(memory_space=pltpu.SMEM),
        ] + [whole(a) for a in consts] + [whole(a) for a in small] + [any_spec] * (1 + len(big)),
        out_specs=[
            pl.BlockSpec((1, PT, ROW_W), tok),
            pl.BlockSpec((SUBLANES, PT), lambda bi, i: (0, bi * (t // PT) + i)),
            pl.BlockSpec((1, LANES), z2),
            any_spec,
            pl.BlockSpec((1, HALO, D), lambda bi, i: (bi, 0, 0)),
        ],
        out_shape=[
            jax.ShapeDtypeStruct((b, t, ROW_W), F32),
            jax.ShapeDtypeStruct((SUBLANES, b * t), jnp.int32),
            jax.ShapeDtypeStruct((1, LANES), F32),
            jax.ShapeDtypeStruct((b, N_HEADS, DK, DV), F32),
            jax.ShapeDtypeStruct((b, HALO, D), F32),
        ],
        scratch_shapes=[pltpu.VMEM(w.shape, BF) for w in big] + [
            pltpu.SemaphoreType.DMA((len(big),)),
            pltpu.SemaphoreType.DMA,
            pltpu.VMEM((2, SUBLANES, CONV_COLS // LANES, HALO + PT, LANES), F32),
            pltpu.VMEM((HALO, D), F32),
            pltpu.VMEM((N_HEADS, DK, DV), F32),
            pltpu.VMEM((PT, D), F32),
            pltpu.VMEM((PT, D_V), F32),
            pltpu.VMEM((PT, D), BF),
            pltpu.VMEM((1, LANES), F32),
        ],
        compiler_params=pltpu.CompilerParams(
            dimension_semantics=("arbitrary", "arbitrary"), vmem_limit_bytes=VMEM_LIMIT),
        name=name,
    )(x, buf0, cos, sin, dmat, dq, dk, gl, *consts, *small, s0, *big)


SEQ_BLK = 2
MXU_ROWS = 16


def _sample_seq_step(step, proj_ref, s_ref, sc_ref, cos_ref, sin_ref, gam_ref, cw_ref, cb_ref,
                     o_ref, c_ref, s_out_ref, sc_out_ref):
    cos = cos_ref[...]
    sin = sin_ref[...]
    row = lax.broadcasted_iota(jnp.int32, (MXU_ROWS, DK), 0)
    for s in range(SEQ_BLK):
        seq = pl.ds(step * SEQ_BLK + s, 1)
        us = proj_ref[seq, O_GLU_A:O_GLU_A + D] * _sigmoid(proj_ref[seq, O_GLU_B:O_GLU_B + D])
        c_ref[seq, :] = (
            jnp.sum(cw_ref[0:CONV_W - 1, :] * sc_ref[s], axis=0, keepdims=True)
            + cw_ref[CONV_W - 1:CONV_W, :] * us + cb_ref[...])
        sc_out_ref[s, 0:CONV_W - 2, :] = sc_ref[s, 1:CONV_W - 1, :]
        sc_out_ref[s, CONV_W - 2:CONV_W - 1, :] = us
        for h in range(N_HEADS):
            q = _rotary(proj_ref[seq, O_Q + h * DK:O_Q + (h + 1) * DK], cos, sin)
            k = _rotary(proj_ref[seq, O_K + h * DK:O_K + (h + 1) * DK], cos, sin) * (DK ** -0.5)
            v = proj_ref[seq, O_V + h * DV:O_V + (h + 1) * DV]
            gam = gam_ref[h]
            qk = jnp.sum(q * k, axis=-1, keepdims=True)
            s_old = s_ref[s, h]
            o8 = _bdot(jnp.broadcast_to(q * gam, (MXU_ROWS, DK)), s_old)
            o_ref[seq, h * DV:(h + 1) * DV] = qk * v + o8[0:1, :]
            k8 = jnp.where(row == 0, jnp.broadcast_to(k, (MXU_ROWS, DK)), 0.0).astype(BF)
            v8 = jnp.broadcast_to(v, (MXU_ROWS, DV)).astype(BF)
            kv = lax.dot_general(k8, v8, (((0,), (0,)), ((), ())), preferred_element_type=F32)
            s_out_ref[s, h] = gam * s_old + kv


def _post_kernel(c_ref, o_ref, gret_ref, gates_ref, x_ref, *rest):
    p = dict(zip(_POST_NAMES, rest[:len(_POST_NAMES)]))
    x1_ref, comb_ref = rest[len(_POST_NAMES):]
    bg = p["b_gates"][...]
    x1, comb, _ = _post_tokens(c_ref[...], o_ref[...], _silu(gret_ref[...]),
                               _sigmoid(gates_ref[:, :D] + bg[:, :D]), _sigmoid(gates_ref[:, D:] + bg[:, D:]),
                               x_ref[...], p)
    x1_ref[...] = x1
    comb_ref[...] = comb


def _post_call(c, o, proj_s, x_s, post_args):
    n = x_s.shape[0]
    z2 = lambda i: (0, 0)
    return pl.pallas_call(
        _post_kernel,
        grid=(1,),
        in_specs=[
            pl.BlockSpec((n, D), z2),
            pl.BlockSpec((n, D_V), z2),
            pl.BlockSpec((n, D_V), lambda i: (0, O_GRET // D_V)),
            pl.BlockSpec((n, 2 * D), lambda i: (0, O_GA // (2 * D))),
            pl.BlockSpec((n, D), z2),
        ] + _post_specs(1),
        out_specs=[pl.BlockSpec((n, D), z2), pl.BlockSpec((n, LANES), z2)],
        out_shape=[jax.ShapeDtypeStruct((n, D), F32), jax.ShapeDtypeStruct((n, LANES), F32)],
        compiler_params=pltpu.CompilerParams(
            dimension_semantics=("arbitrary",), vmem_limit_bytes=VMEM_LIMIT),
        name="sample_post",
    )(c, o, proj_s, proj_s, x_s, *post_args)


def _moe_kernel(x1_ref, comb_ref, fg_ref, ng_ref, wg_ref, wu_ref, wd_ref, y_ref, t_ref, acc_ref):
    e = pl.program_id(1)

    @pl.when(e == 0)
    def _():
        t_ref[...] = _rmsnorm(x1_ref[...], fg_ref[...]).astype(BF)
        acc_ref[...] = jnp.zeros_like(acc_ref)

    t = t_ref[...]
    hg = jnp.dot(t, wg_ref[0], preferred_element_type=F32)
    hu = jnp.dot(t, wu_ref[0], preferred_element_type=F32)
    comb = comb_ref[...]
    lane = lax.broadcasted_iota(jnp.int32, comb.shape, 1)
    ce = jnp.sum(jnp.where(lane == e, comb, 0.0), axis=-1, keepdims=True)
    act = (hg * _sigmoid(hg)) * hu * ce
    acc_ref[...] += jnp.dot(act.astype(BF), wd_ref[0], preferred_element_type=F32)

    @pl.when(e == N_EXPERTS - 1)
    def _():
        y_ref[...] = _rmsnorm(x1_ref[...] + acc_ref[...], ng_ref[...])


def _moe_call(x1, comb, ffn_g, final_g, wg, wu, wd, tm):
    n = x1.shape[0]
    z2 = lambda i, e: (0, 0)
    return pl.pallas_call(
        _moe_kernel,
        grid=(n // tm, N_EXPERTS),
        in_specs=[
            pl.BlockSpec((tm, D), lambda i, e: (i, 0)),
            pl.BlockSpec((tm, LANES), lambda i, e: (i, 0)),
            pl.BlockSpec((1, D), z2),
            pl.BlockSpec((1, D), z2),
            pl.BlockSpec((1, D, D_EXPERT), lambda i, e: (e, 0, 0)),
            pl.BlockSpec((1, D, D_EXPERT), lambda i, e: (e, 0, 0)),
            pl.BlockSpec((1, D_EXPERT, D), lambda i, e: (e, 0, 0)),
        ],
        out_specs=pl.BlockSpec((tm, D), lambda i, e: (i, 0)),
        out_shape=jax.ShapeDtypeStruct((n, D), F32),
        scratch_shapes=[pltpu.VMEM((tm, D), BF), pltpu.VMEM((tm, D), F32)],
        compiler_params=pltpu.CompilerParams(
            dimension_semantics=("arbitrary", "arbitrary"), vmem_limit_bytes=VMEM_LIMIT),
        name="moe",
    )(x1, comb, ffn_g, final_g, wg, wu, wd)


def _moe_sparse_kernel(n_seq_steps, slot_ref, e_lo_ref, e_hi_ref, nvalid_ref, nused_ref,
                       rows_hbm, fg_ref, ng_ref, wg_lo, wu_lo, wd_lo, wg_hi, wu_hi, wd_hi,
                       dproj_ref, ds_ref, dsc_ref, dcos_ref, dsin_ref, dgam_ref, dcw_ref, dcb_ref,
                       y_hbm, do_ref, dc_ref, ds_out_ref, dsc_out_ref,
                       inv_ref, xbuf, ybuf, gsem, ssem):
    t = pl.program_id(0)
    n_used = nused_ref[0]
    n_tok = slot_ref.shape[0]
    cur = lax.rem(t, 2)

    @pl.when(t < n_seq_steps)
    def _():
        _sample_seq_step(t, dproj_ref, ds_ref, dsc_ref, dcos_ref, dsin_ref, dgam_ref, dcw_ref, dcb_ref,
                         do_ref, dc_ref, ds_out_ref, dsc_out_ref)

    def hbm_row(ref, tok):
        return ref.at[jnp.right_shift(tok, 3), pl.ds(tok & (SUBLANES - 1), 1), :]

    def gather_copy(grp, sub, tok, buf):
        return pltpu.make_async_copy(hbm_row(rows_hbm, tok), xbuf.at[buf, grp, pl.ds(sub, 1), :],
                                     gsem.at[buf])

    def scatter_copy(grp, sub, tok, buf):
        return pltpu.make_async_copy(ybuf.at[buf, grp, pl.ds(sub, 1), :], hbm_row(y_hbm, tok),
                                     ssem.at[buf])

    def start_rows(copy, tile, buf):
        base = tile * MOE_TM
        n_rows = nvalid_ref[tile]
        n_grp = n_rows // SUBLANES

        def group(g, c):
            for u in range(SUBLANES):
                copy(g, u, inv_ref[base + g * SUBLANES + u], buf).start()
            return c
        lax.fori_loop(0, n_grp, group, 0)

        def tail(r, c):
            copy(n_grp, r - n_grp * SUBLANES, inv_ref[base + r], buf).start()
            return c
        lax.fori_loop(n_grp * SUBLANES, n_rows, tail, 0)

    def wait_rows(copy, tile, buf):
        n_rows = nvalid_ref[tile]
        vbuf, hbm, sem = (xbuf, rows_hbm, gsem) if copy is gather_copy else (ybuf, y_hbm, ssem)
        k = MOE_TM
        while k >= 1:
            @pl.when((n_rows & k) != 0)
            def _(k=k):
                if k >= SUBLANES:
                    vmem = vbuf.at[buf, pl.ds(0, k // SUBLANES)]
                    ext = hbm.at[pl.ds(0, k // SUBLANES)]
                else:
                    vmem = vbuf.at[buf, 0, pl.ds(0, k), :]
                    ext = hbm.at[0, pl.ds(0, k), :]
                pair = (ext, vmem) if copy is gather_copy else (vmem, ext)
                pltpu.make_async_copy(pair[0], pair[1], sem.at[buf]).wait()
            k //= 2

    @pl.when(t == 0)
    def _():
        def fill(tok, c):
            inv_ref[slot_ref[tok]] = tok
            return c
        lax.fori_loop(0, n_tok, fill, 0, unroll=DMA_UNROLL)
        xbuf[...] = jnp.zeros_like(xbuf)
        start_rows(gather_copy, 0, 0)

    @pl.when(t < n_used)
    def _():
        wait_rows(gather_copy, t, cur)

        @pl.when(t + 1 < n_used)
        def _():
            start_rows(gather_copy, t + 1, 1 - cur)

        xt = xbuf[cur].reshape(MOE_TM, ROW_W)
        x1 = xt[:, 0:D]
        aux = xt[:, D:ROW_W]
        tb = _rmsnorm(x1, fg_ref[...]).astype(BF)
        acc = None
        for wg, wu, wd, lane_w in ((wg_lo, wu_lo, wd_lo, AUX_W_LO), (wg_hi, wu_hi, wd_hi, AUX_W_HI)):
            hg = jnp.dot(tb, wg[0], preferred_element_type=F32)
            hu = jnp.dot(tb, wu[0], preferred_element_type=F32)
            act = (hg * _sigmoid(hg)) * hu * aux[:, lane_w:lane_w + 1]
            part = jnp.dot(act.astype(BF), wd[0], preferred_element_type=F32)
            acc = part if acc is None else acc + part
        y = _rmsnorm(x1 + acc, ng_ref[...])

        @pl.when(t >= 2)
        def _():
            wait_rows(scatter_copy, jnp.maximum(t - 2, 0), cur)
        ybuf[cur] = y.reshape(MOE_TM // SUBLANES, SUBLANES, D)
        start_rows(scatter_copy, t, cur)

        @pl.when(t == n_used - 1)
        def _():
            wait_rows(scatter_copy, t, cur)

            @pl.when(t >= 1)
            def _():
                wait_rows(scatter_copy, jnp.maximum(t - 1, 0), 1 - cur)


def _bin_tables():
    lo, hi = [], []
    for g in range(N_GROUPS):
        for a in range(EPG):
            for b in range(a + 1, EPG):
                lo.append(g * EPG + a)
                hi.append(g * EPG + b)
    return jnp.array(lo, jnp.int32), jnp.array(hi, jnp.int32)


def _moe_sparse_call(rows, route, counts, ffn_g, final_g, wg, wu, wd, decode):
    proj_s, state_ret, state_conv, cos_s, sin_s, gam, cw, cb = decode
    n_seq = proj_s.shape[0]
    n_seq_steps = n_seq // SEQ_BLK
    n = route.shape[1]
    max_tiles = n // MOE_TM + N_BINS
    assert max_tiles >= n_seq_steps
    padded = ((counts + MOE_TM - 1) // MOE_TM) * MOE_TM
    ends = jnp.cumsum(padded)
    offs = ends - padded
    n_used = ends[-1] // MOE_TM
    slot = offs[route[ROUTE_BIN]] + route[ROUTE_RANK]
    tile_start = jnp.minimum(jnp.arange(max_tiles, dtype=jnp.int32), n_used - 1) * MOE_TM
    tile_bin = jnp.sum((tile_start[:, None] >= ends[None, :]).astype(jnp.int32), axis=1)
    bin_lo, bin_hi = _bin_tables()
    e_lo = bin_lo[tile_bin]
    e_hi = bin_hi[tile_bin]
    tile_id = jnp.arange(max_tiles, dtype=jnp.int32)
    n_valid = jnp.where(tile_id < n_used,
                        jnp.clip(counts[tile_bin] - (tile_start - offs[tile_bin]), 0, MOE_TM), 0)

    z2 = lambda t, *_: (0, 0)
    w_lo_map = lambda t, slot_r, lo_r, hi_r, nv_r, nu_r: (lo_r[t], 0, 0)
    w_hi_map = lambda t, slot_r, lo_r, hi_r, nv_r, nu_r: (hi_r[t], 0, 0)
    seq3 = lambda t, *_: (jnp.minimum(t, n_seq_steps - 1), 0, 0)
    seq4 = lambda t, *_: (jnp.minimum(t, n_seq_steps - 1), 0, 0, 0)
    y, o, c, s_new, sc_new = pl.pallas_call(
        functools.partial(_moe_sparse_kernel, n_seq_steps),
        grid_spec=pltpu.PrefetchScalarGridSpec(
            num_scalar_prefetch=5,
            grid=(max_tiles,),
            in_specs=[
                pl.BlockSpec(memory_space=pl.ANY),
                pl.BlockSpec((1, D), z2),
                pl.BlockSpec((1, D), z2),
                pl.BlockSpec((1, D, D_EXPERT), w_lo_map),
                pl.BlockSpec((1, D, D_EXPERT), w_lo_map),
                pl.BlockSpec((1, D_EXPERT, D), w_lo_map),
                pl.BlockSpec((1, D, D_EXPERT), w_hi_map),
                pl.BlockSpec((1, D, D_EXPERT), w_hi_map),
                pl.BlockSpec((1, D_EXPERT, D), w_hi_map),
                pl.BlockSpec((n_seq, D_IN), z2),
                pl.BlockSpec((SEQ_BLK, N_HEADS, DK, DV), seq4),
                pl.BlockSpec((SEQ_BLK, CONV_W - 1, D), seq3),
                pl.BlockSpec((1, LANES), z2),
                pl.BlockSpec((1, LANES), z2),
                pl.BlockSpec(memory_space=pltpu.SMEM),
                pl.BlockSpec((CONV_W, D), z2),
                pl.BlockSpec((1, D), z2),
            ],
            out_specs=[
                pl.BlockSpec(memory_space=pl.ANY),
                pl.BlockSpec((n_seq, D_V), z2),
                pl.BlockSpec((n_seq, D), z2),
                pl.BlockSpec((SEQ_BLK, N_HEADS, DK, DV), seq4),
                pl.BlockSpec((SEQ_BLK, CONV_W - 1, D), seq3),
            ],
            scratch_shapes=[
                pltpu.SMEM((max_tiles * MOE_TM,), jnp.int32),
                pltpu.VMEM((2, MOE_TM // SUBLANES, SUBLANES, ROW_W), F32),
                pltpu.VMEM((2, MOE_TM // SUBLANES, SUBLANES, D), F32),
                pltpu.SemaphoreType.DMA((2,)),
                pltpu.SemaphoreType.DMA((2,)),
            ],
        ),
        out_shape=[
            jax.ShapeDtypeStruct((n // SUBLANES, SUBLANES, D), F32),
            jax.ShapeDtypeStruct((n_seq, D_V), F32),
            jax.ShapeDtypeStruct((n_seq, D), F32),
            jax.ShapeDtypeStruct(state_ret.shape, F32),
            jax.ShapeDtypeStruct(state_conv.shape, F32),
        ],
        compiler_params=pltpu.CompilerParams(
            dimension_semantics=("arbitrary",), vmem_limit_bytes=VMEM_LIMIT),
        name="moe_sparse",
    )(slot, e_lo, e_hi, n_valid, n_used.reshape(1).astype(jnp.int32),
      rows.reshape(n // SUBLANES, SUBLANES, ROW_W), ffn_g, final_g, wg, wu, wd, wg, wu, wd,
      proj_s, state_ret, state_conv, cos_s, sin_s, gam, cw, cb)
    return y.reshape(n, D), o, c, s_new, sc_new


def _rope_tables(pos):
    half = DK // 2
    freqs = np.power(np.float64(ROPE_BASE), -np.arange(half, dtype=np.float64) / half)
    ang = pos.astype(np.float64)[:, None] * freqs[None, :]
    return jnp.asarray(np.cos(ang), F32), jnp.asarray(np.sin(ang), F32)


def _decay_tables(L):
    lg = np.log(1.0 - np.exp2(-5.0 - np.arange(N_HEADS, dtype=np.float64)))
    idx = np.arange(L, dtype=np.float64)
    diff = idx[:, None] - idx[None, :]
    dmat = np.where((diff >= 0)[None], np.exp(np.maximum(diff, 0.0)[None] * lg[:, None, None]), 0.0)
    dq = np.exp((idx[:, None] + 1.0) * lg[None, :]).T
    dk = np.exp((L - 1.0 - idx)[:, None] * lg[None, :]).T
    gl = np.exp(L * lg)
    bl = lambda a: jnp.asarray(np.broadcast_to(a[:, :, None], (N_HEADS, L, LANES)), F32)
    return jnp.asarray(dmat, F32), bl(dq), bl(dk), jnp.asarray(gl, F32)


def kernel(x_prompt, x_sample, state_ret, state_conv, meta_tokens, norm_mix_g, w_in, b_gates, conv_w, conv_b, conv_ln_g, conv_ln_b, w_conv_out, ret_gn_g, w_ret_out, w_o, norm_ffn_g, w_coarse, b_coarse, w_fine, b_fine, w_gate_e, w_up_e, w_down_e, norm_final_g):
    bp, seq, _ = x_prompt.shape
    ns = x_sample.shape[0]
    l = 0
    row = lambda a: a.reshape(1, -1)

    w_in_bf = w_in[l].astype(BF)
    wr = jnp.zeros((D, LANES), F32).at[:, :N_EXPERTS].set(w_fine[l]).at[:, N_EXPERTS:N_EXPERTS + N_GROUPS].set(w_coarse[l])
    wr_hi = wr.astype(BF)
    wr_lo = (wr - wr_hi.astype(F32)).astype(BF)
    br = jnp.zeros((1, LANES), F32).at[0, :N_EXPERTS].set(b_fine[l]).at[0, N_EXPERTS:N_EXPERTS + N_GROUPS].set(b_coarse[l])
    post_args = (row(conv_ln_g[l]), row(conv_ln_b[l]), w_conv_out[l].astype(BF), row(ret_gn_g[l]),
                 w_ret_out[l].astype(BF), row(b_gates[l]), w_o[l].astype(BF), row(norm_ffn_g[l]),
                 wr_hi, wr_lo, br)
    wg = w_gate_e[l].astype(BF)
    wu = w_up_e[l].astype(BF)
    wd = w_down_e[l].astype(BF)
    mix_g = row(norm_mix_g[l])
    cw = conv_w[l]
    cb = row(conv_b[l])

    small = dict(zip(_POST_NAMES, post_args))
    small_args = tuple(small[k] for k in _SMALL_NAMES)
    big_args = (w_in_bf,) + tuple(small[k] for k in _BIG_NAMES[1:])
    r = np.arange(PT)
    tri = jnp.asarray(r[None, :] < r[:, None], BF)
    consts = (cw, cb, tri, mix_g)
    decay = _decay_tables(PT)
    x_meta = jnp.concatenate([jnp.zeros((PT - N_META, D), F32), meta_tokens])[None]
    pos_meta = np.maximum(np.arange(PT) - (PT - N_META), 0)
    _, _, _, s_meta, buf_meta = _prompt_call(
        x_meta, jnp.zeros((N_HEADS, DK, DV), F32), jnp.zeros((HALO, D), F32),
        _rope_tables(pos_meta) + decay, consts, small_args, big_args, "meta_state")
    pos_p = N_META + np.arange(seq)
    rows_p, route_p, counts, s_new_p, buf_p = _prompt_call(
        x_prompt, s_meta[0], buf_meta[0], _rope_tables(pos_p) + decay, consts, small_args, big_args,
        "prompt_mix")

    xs = x_sample.reshape(ns, D)
    proj_s = _proj_call(xs, mix_g, w_in_bf, tm=ns)
    pos_s = np.full((1,), PAST_LEN)
    cos_s, sin_s = _rope_tables(pos_s)
    _, _, _, gam = _decay_tables(1)

    y_p, o_s, c_s, s_new_s, buf_s = _moe_sparse_call(
        rows_p.reshape(bp * seq, ROW_W), route_p,
        counts[0, :N_BINS].astype(jnp.int32), row(norm_ffn_g[l]), row(norm_final_g), wg, wu, wd,
        (proj_s, state_ret[l], state_conv[l], cos_s, sin_s, gam, cw, cb))

    x1_s, comb_s = _post_call(c_s, o_s, proj_s, xs, post_args)
    y_s = _moe_call(x1_s, comb_s, row(norm_ffn_g[l]), row(norm_final_g), wg, wu, wd, tm=ns).reshape(ns, 1, D)

    return (y_p.reshape(bp, seq, D), y_s, s_new_p[None], buf_p[:, HALO - (CONV_W - 1):][None],
            s_new_s[None], buf_s[None])
```

```python
import functools

import jax
import jax.numpy as jnp
import numpy as np
from jax import lax
from jax.experimental import pallas as pl
from jax.experimental.pallas import tpu as pltpu

D = 1024
N_META = 16
PAST_LEN = 16384
CONV_W = 31
N_HEADS = 4
DK = 256
DV = 512
D_QK = N_HEADS * DK
D_V = N_HEADS * DV
ROPE_BASE = 10000.0
N_GROUPS = 4
EPG = 4
N_EXPERTS = 16
D_EXPERT = 512
EPS = 1e-6
D_IN = 2 * D + 2 * D_QK + 2 * D_V + 2 * D

O_GLU_A, O_GLU_B = 0, D
O_Q, O_K = 2 * D, 2 * D + D_QK
O_V = 2 * D + 2 * D_QK
O_GRET = O_V + D_V
O_GA = O_GRET + D_V
O_GB = O_GA + D

N_PAIRS = EPG * (EPG - 1) // 2
N_BINS = N_GROUPS * N_PAIRS

LANES = 128
SUBLANES = 8
MXU_DIM = 256
ROW_W = D + LANES
AUX_W_LO, AUX_W_HI = 0, 1
ROUTE_BIN, ROUTE_RANK = 0, 1
MOE_TM = MXU_DIM
FILL_UNROLL = 8
PT = MXU_DIM
CONV_COLS = MXU_DIM
HALO = 32
VMEM_LIMIT = 56 * 1024 * 1024

BF = jnp.bfloat16
F32 = jnp.float32


def _sigmoid(x):
    return 1.0 / (1.0 + jnp.exp(-x))


def _bdot(a, b):
    return jnp.dot(a.astype(BF), b.astype(BF), preferred_element_type=F32)


def _rmsnorm(x, g):
    return x * lax.rsqrt(jnp.mean(x * x, axis=-1, keepdims=True) + EPS) * g


def _proj_kernel(x_ref, g_ref, w_ref, o_ref):
    h = _rmsnorm(x_ref[...], g_ref[...])
    o_ref[...] = jnp.dot(h.astype(BF), w_ref[...], preferred_element_type=F32)


def _proj_call(x2d, g, w_bf, tm, tn=2048):
    n = x2d.shape[0]
    return pl.pallas_call(
        _proj_kernel,
        grid=(D_IN // tn, n // tm),
        in_specs=[
            pl.BlockSpec((tm, D), lambda j, i: (i, 0)),
            pl.BlockSpec((1, D), lambda j, i: (0, 0)),
            pl.BlockSpec((D, tn), lambda j, i: (0, j)),
        ],
        out_specs=pl.BlockSpec((tm, tn), lambda j, i: (i, j)),
        out_shape=jax.ShapeDtypeStruct((n, D_IN), F32),
        compiler_params=pltpu.CompilerParams(
            dimension_semantics=("arbitrary", "arbitrary"), vmem_limit_bytes=VMEM_LIMIT),
        name="in_proj",
    )(x2d, g, w_bf)


def _route(t, wr_hi_ref, wr_lo_ref, br_ref):
    t_hi = t.astype(BF)
    t_lo = (t - t_hi.astype(F32)).astype(BF)
    wh = wr_hi_ref[...]
    logits = (jnp.dot(t_hi, wh, preferred_element_type=F32)
              + jnp.dot(t_hi, wr_lo_ref[...], preferred_element_type=F32)
              + jnp.dot(t_lo, wh, preferred_element_type=F32)) + br_ref[...]
    lt = logits.T
    row = lambda r: lt[r:r + 1, :]
    neg = jnp.float32(-jnp.inf)
    cmax, g_sel = _first_max([row(N_EXPERTS + k) for k in range(N_GROUPS)])
    csum = sum(jnp.exp(row(N_EXPERTS + k) - cmax) for k in range(N_GROUPS))
    p_g = 1.0 / csum
    fine = []
    for k in range(EPG):
        cand = [row(grp * EPG + k) for grp in range(N_GROUPS)]
        fine.append(jnp.where(g_sel == 0, cand[0], jnp.where(g_sel == 1, cand[1],
                                                             jnp.where(g_sel == 2, cand[2], cand[3]))))
    v1, j1 = _first_max(fine)
    v2, j2 = _first_max([jnp.where(j1 == k, neg, fine[k]) for k in range(EPG)])
    e2 = jnp.exp(v2 - v1)
    w1 = p_g / (1.0 + e2)
    w2 = e2 * w1
    first_low = j1 < j2
    a = jnp.where(first_low, j1, j2)
    b = jnp.where(first_low, j2, j1)
    pair = jnp.where(a == 0, 0, jnp.where(a == 1, 3, 5)) + (b - a - 1)
    bin_id = g_sel * N_PAIRS + pair
    w_lo = jnp.where(first_low, w1, w2)
    w_hi = jnp.where(first_low, w2, w1)
    sub = lax.broadcasted_iota(jnp.int32, lt.shape, 0)
    comb = jnp.where(sub == g_sel * EPG + j1, w1, jnp.where(sub == g_sel * EPG + j2, w2, 0.0)).T
    rec = jnp.where(sub == 0, w_lo, jnp.where(sub == 1, w_hi,
                                              jnp.where(sub == 2, bin_id.astype(F32), 0.0))).T
    return comb, (rec[:, 0:1], rec[:, 1:2], rec[:, 2:3].astype(jnp.int32))


def _first_max(vals):
    b01 = vals[1] > vals[0]
    b23 = vals[3] > vals[2]
    m01 = jnp.where(b01, vals[1], vals[0])
    m23 = jnp.where(b23, vals[3], vals[2])
    top = m23 > m01
    idx = jnp.where(top, jnp.where(b23, 3, 2), jnp.where(b01, 1, 0)).astype(jnp.int32)
    return jnp.where(top, m23, m01), idx


def _silu(x):
    return x * _sigmoid(x)


def _post_tokens(c, o, gret_act, g_a, g_b, x, p):
    mu = jnp.mean(c, axis=-1, keepdims=True)
    dc = c - mu
    var = jnp.mean(dc * dc, axis=-1, keepdims=True)
    cn = dc * lax.rsqrt(var + EPS) * p["ln_g"][...] + p["ln_b"][...]
    y_a = _bdot(_silu(cn), p["w_conv_out"][...])
    parts = []
    for h in range(N_HEADS):
        oh = o[:, h * DV:(h + 1) * DV]
        omu = jnp.mean(oh, axis=-1, keepdims=True)
        od = oh - omu
        ovar = jnp.mean(od * od, axis=-1, keepdims=True)
        parts.append(od * lax.rsqrt(ovar + EPS))
    on = jnp.concatenate(parts, axis=-1) * p["gn_g"][...]
    y_b = _bdot(on * gret_act, p["w_ret_out"][...])
    x1 = x + _bdot(g_a * y_a + g_b * y_b, p["w_o"][...])
    t = _rmsnorm(x1, p["ffn_g"][...])
    comb, pair_route = _route(t, p["wr_hi"], p["wr_lo"], p["br"])
    return x1, comb, pair_route


_POST_NAMES = ("ln_g", "ln_b", "w_conv_out", "gn_g", "w_ret_out", "b_gates", "w_o", "ffn_g",
               "wr_hi", "wr_lo", "br")


def _post_specs():
    z2 = lambda *_: (0, 0)
    shapes = {"ln_g": (1, D), "ln_b": (1, D), "w_conv_out": (D, D), "gn_g": (1, D_V),
              "w_ret_out": (D_V, D), "b_gates": (1, 2 * D), "w_o": (D, D), "ffn_g": (1, D),
              "wr_hi": (D, LANES), "wr_lo": (D, LANES), "br": (1, LANES)}
    return [pl.BlockSpec(shapes[k], z2) for k in _POST_NAMES]


def _rotary(x, cos, sin):
    x1, x2 = x[:, :LANES], x[:, LANES:]
    return jnp.concatenate([x1 * cos - x2 * sin, x1 * sin + x2 * cos], axis=-1)


_SMALL_NAMES = ("ln_g", "ln_b", "gn_g", "b_gates", "ffn_g", "wr_hi", "wr_lo", "br")
_BIG_NAMES = ("w_in", "w_conv_out", "w_ret_out", "w_o")


def _prompt_kernel(x_ref, buf0_ref, cos_ref, sin_ref, dmat_ref, dq_ref, dk_ref, gl_ref,
                   cw_ref, cb_ref, tri_ref, mixg_ref, *rest):
    ns, nb = len(_SMALL_NAMES), len(_BIG_NAMES)
    small = dict(zip(_SMALL_NAMES, rest[:ns]))
    s0_hbm = rest[ns]
    big_hbm = rest[ns + 1:ns + 1 + nb]
    x1e_ref, route_ref, cnt_out_ref, s_out_hbm, buf_out_ref = rest[ns + 1 + nb:ns + 6 + nb]
    scratch = rest[ns + 6 + nb:]
    big = dict(zip(_BIG_NAMES, scratch[:nb]))
    wsem, ssem, shift_ref, tail_ref, s_ref, c_ref, o_ref, h_ref, cnt_ref = scratch[nb:]
    p = dict(small, **{k: big[k] for k in _BIG_NAMES[1:]})
    w_in = big["w_in"]
    b = pl.program_id(0)
    i = pl.program_id(1)

    @pl.when((b == 0) & (i == 0))
    def _():
        copies = [pltpu.make_async_copy(src, big[k], wsem.at[n])
                  for n, (k, src) in enumerate(zip(_BIG_NAMES, big_hbm))]
        for cp in copies:
            cp.start()
        for cp in copies:
            cp.wait()
        cnt_ref[...] = jnp.zeros_like(cnt_ref)

    @pl.when(i == 0)
    def _():
        load_state = pltpu.make_async_copy(s0_hbm, s_ref, ssem)
        load_state.start()
        load_state.wait()
        tail_ref[...] = buf0_ref[...]

    h_ref[...] = _rmsnorm(x_ref[0], mixg_ref[...]).astype(BF)

    def proj(lo, width):
        return jnp.dot(h_ref[...], w_in[:, lo:lo + width], preferred_element_type=F32)

    base = HALO - (CONV_W - 1)

    def conv_block(cb):
        c0 = cb * CONV_COLS
        par = cb % 2
        u = proj(O_GLU_A + c0, CONV_COLS) * _sigmoid(proj(O_GLU_B + c0, CONV_COLS))
        for half in range(CONV_COLS // LANES):
            cols = slice(c0 + half * LANES, c0 + (half + 1) * LANES)
            uh = u[:, half * LANES:(half + 1) * LANES]
            for s in range(SUBLANES):
                shift_ref[par, s, half, 0:HALO - s, :] = tail_ref[s:HALO, cols]
                shift_ref[par, s, half, HALO - s:HALO - s + PT, :] = uh
            tail_ref[:, cols] = uh[PT - HALO:PT, :]
            acc = jnp.broadcast_to(cb_ref[:, cols], (PT, LANES))
            for j in range(CONV_W):
                s = (base + j) % SUBLANES
                start = base + j - s
                acc = acc + cw_ref[j:j + 1, cols] * shift_ref[par, s, half, start:start + PT, :]
            c_ref[:, cols] = acc

    cos = cos_ref[...]
    sin = sin_ref[...]

    def retention_head(h):
        q = _rotary(proj(O_Q + h * DK, DK), cos, sin)
        k = _rotary(proj(O_K + h * DK, DK), cos, sin) * (DK ** -0.5)
        v = proj(O_V + h * DV, DV).astype(BF)
        dq = dq_ref[h]
        dk = dk_ref[h]
        scores = lax.dot_general(q.astype(BF), k.astype(BF), (((1,), (1,)), ((), ())),
                                 preferred_element_type=F32) * dmat_ref[h]
        s_old = s_ref[h]
        q_dec = jnp.concatenate([q[:, :LANES] * dq, q[:, LANES:] * dq], axis=-1)
        k_dec = jnp.concatenate([k[:, :LANES] * dk, k[:, LANES:] * dk], axis=-1)
        o_ref[:, h * DV:(h + 1) * DV] = (jnp.dot(scores.astype(BF), v, preferred_element_type=F32)
                                         + _bdot(q_dec, s_old))
        s_ref[h] = gl_ref[h] * s_old + lax.dot_general(
            k_dec.astype(BF), v, (((0,), (0,)), ((), ())), preferred_element_type=F32)

    assert D // CONV_COLS == N_HEADS
    for n in range(N_HEADS):
        conv_block(n)
        retention_head(n)

    @pl.when(i == pl.num_programs(1) - 1)
    def _():
        store_state = pltpu.make_async_copy(s_ref, s_out_hbm.at[b], ssem)
        store_state.start()
        store_state.wait()
        buf_out_ref[0] = tail_ref[...]

    bg = small["b_gates"]
    x1, _, (w_lo, w_hi, bin_id) = _post_tokens(c_ref[...], o_ref[...], _silu(proj(O_GRET, D_V)),
                                               _sigmoid(proj(O_GA, D) + bg[:, :D]),
                                               _sigmoid(proj(O_GB, D) + bg[:, D:]), x_ref[0], p)

    lane = lax.broadcasted_iota(jnp.int32, (PT, LANES), 1)
    onehot = lane == bin_id
    earlier = jnp.dot(tri_ref[...], onehot.astype(BF), preferred_element_type=F32)
    rank = jnp.sum(jnp.where(onehot, earlier + cnt_ref[...], 0.0), axis=-1, keepdims=True)
    cnt_ref[...] += jnp.sum(onehot.astype(F32), axis=0, keepdims=True)

    x1e_ref[0, :, 0:D] = x1
    x1e_ref[0, :, D:ROW_W] = jnp.where(lane == AUX_W_LO, w_lo, jnp.where(lane == AUX_W_HI, w_hi, 0.0))
    record = jnp.where(lane == ROUTE_BIN, bin_id, jnp.where(lane == ROUTE_RANK, rank.astype(jnp.int32), 0))
    route_ref[...] = record.T[0:SUBLANES, :]
    cnt_out_ref[...] = cnt_ref[...]


def _prompt_call(x, s0, buf0, tabs, consts, small, big, name):
    b, t, _ = x.shape
    cos, sin, dmat, dq, dk, gl = tabs
    z2 = lambda bi, i: (0, 0)
    z3 = lambda bi, i: (0, 0, 0)
    tok = lambda bi, i: (bi, i, 0)
    whole = lambda a: pl.BlockSpec(a.shape, z2 if a.ndim == 2 else z3)
    any_spec = pl.BlockSpec(memory_space=pl.ANY)
    return pl.pallas_call(
        _prompt_kernel,
        grid=(b, t // PT),
        in_specs=[
            pl.BlockSpec((1, PT, D), tok),
            whole(buf0),
            pl.BlockSpec((PT, LANES), lambda bi, i: (i, 0)),
            pl.BlockSpec((PT, LANES), lambda bi, i: (i, 0)),
            whole(dmat), whole(dq), whole(dk),
            pl.BlockSpec(memory_space=pltpu.SMEM),
        ] + [whole(a) for a in consts] + [whole(a) for a in small] + [any_spec] * (1 + len(big)),
        out_specs=[
            pl.BlockSpec((1, PT, ROW_W), tok),
            pl.BlockSpec((SUBLANES, PT), lambda bi, i: (0, bi * (t // PT) + i)),
            pl.BlockSpec((1, LANES), z2),
            any_spec,
            pl.BlockSpec((1, HALO, D), lambda bi, i: (bi, 0, 0)),
        ],
        out_shape=[
            jax.ShapeDtypeStruct((b, t, ROW_W), F32),
            jax.ShapeDtypeStruct((SUBLANES, b * t), jnp.int32),
            jax.ShapeDtypeStruct((1, LANES), F32),
            jax.ShapeDtypeStruct((b, N_HEADS, DK, DV), F32),
            jax.ShapeDtypeStruct((b, HALO, D), F32),
        ],
        scratch_shapes=[pltpu.VMEM(w.shape, BF) for w in big] + [
            pltpu.SemaphoreType.DMA((len(big),)),
            pltpu.SemaphoreType.DMA,
            pltpu.VMEM((2, SUBLANES, CONV_COLS // LANES, HALO + PT, LANES), F32),
            pltpu.VMEM((HALO, D), F32),
            pltpu.VMEM((N_HEADS, DK, DV), F32),
            pltpu.VMEM((PT, D), F32),
            pltpu.VMEM((PT, D_V), F32),
            pltpu.VMEM((PT, D), BF),
            pltpu.VMEM((1, LANES), F32),
        ],
        compiler_params=pltpu.CompilerParams(
            dimension_semantics=("arbitrary", "arbitrary"), vmem_limit_bytes=VMEM_LIMIT),
        name=name,
    )(x, buf0, cos, sin, dmat, dq, dk, gl, *consts, *small, s0, *big)


SEQ_BLK = 2
MXU_ROWS = 16


def _sample_seq_step(step, proj_ref, s_ref, sc_ref, cos_ref, sin_ref, gam_ref, cw_ref, cb_ref,
                     o_ref, c_ref, s_out_ref, sc_out_ref):
    cos = cos_ref[...]
    sin = sin_ref[...]
    row = lax.broadcasted_iota(jnp.int32, (MXU_ROWS, DK), 0)
    for s in range(SEQ_BLK):
        seq = pl.ds(step * SEQ_BLK + s, 1)
        us = proj_ref[seq, O_GLU_A:O_GLU_A + D] * _sigmoid(proj_ref[seq, O_GLU_B:O_GLU_B + D])
        c_ref[seq, :] = (
            jnp.sum(cw_ref[0:CONV_W - 1, :] * sc_ref[s], axis=0, keepdims=True)
            + cw_ref[CONV_W - 1:CONV_W, :] * us + cb_ref[...])
        sc_out_ref[s, 0:CONV_W - 2, :] = sc_ref[s, 1:CONV_W - 1, :]
        sc_out_ref[s, CONV_W - 2:CONV_W - 1, :] = us
        for h in range(N_HEADS):
            q = _rotary(proj_ref[seq, O_Q + h * DK:O_Q + (h + 1) * DK], cos, sin)
            k = _rotary(proj_ref[seq, O_K + h * DK:O_K + (h + 1) * DK], cos, sin) * (DK ** -0.5)
            v = proj_ref[seq, O_V + h * DV:O_V + (h + 1) * DV]
            gam = gam_ref[h]
            qk = jnp.sum(q * k, axis=-1, keepdims=True)
            s_old = s_ref[s, h]
            o8 = _bdot(jnp.broadcast_to(q * gam, (MXU_ROWS, DK)), s_old)
            o_ref[seq, h * DV:(h + 1) * DV] = qk * v + o8[0:1, :]
            k8 = jnp.where(row == 0, jnp.broadcast_to(k, (MXU_ROWS, DK)), 0.0).astype(BF)
            v8 = jnp.broadcast_to(v, (MXU_ROWS, DV)).astype(BF)
            kv = lax.dot_general(k8, v8, (((0,), (0,)), ((), ())), preferred_element_type=F32)
            s_out_ref[s, h] = gam * s_old + kv


def _post_kernel(c_ref, o_ref, gret_ref, gates_ref, x_ref, *rest):
    p = dict(zip(_POST_NAMES, rest[:len(_POST_NAMES)]))
    x1_ref, comb_ref = rest[len(_POST_NAMES):]
    bg = p["b_gates"][...]
    x1, comb, _ = _post_tokens(c_ref[...], o_ref[...], _silu(gret_ref[...]),
                               _sigmoid(gates_ref[:, :D] + bg[:, :D]), _sigmoid(gates_ref[:, D:] + bg[:, D:]),
                               x_ref[...], p)
    x1_ref[...] = x1
    comb_ref[...] = comb


def _post_call(c, o, proj_s, x_s, post_args):
    n = x_s.shape[0]
    z2 = lambda i: (0, 0)
    return pl.pallas_call(
        _post_kernel,
        grid=(1,),
        in_specs=[
            pl.BlockSpec((n, D), z2),
            pl.BlockSpec((n, D_V), z2),
            pl.BlockSpec((n, D_V), lambda i: (0, O_GRET // D_V)),
            pl.BlockSpec((n, 2 * D), lambda i: (0, O_GA // (2 * D))),
            pl.BlockSpec((n, D), z2),
        ] + _post_specs(),
        out_specs=[pl.BlockSpec((n, D), z2), pl.BlockSpec((n, LANES), z2)],
        out_shape=[jax.ShapeDtypeStruct((n, D), F32), jax.ShapeDtypeStruct((n, LANES), F32)],
        compiler_params=pltpu.CompilerParams(
            dimension_semantics=("arbitrary",), vmem_limit_bytes=VMEM_LIMIT),
        name="sample_post",
    )(c, o, proj_s, proj_s, x_s, *post_args)


def _moe_kernel(x1_ref, comb_ref, fg_ref, ng_ref, wg_ref, wu_ref, wd_ref, y_ref, t_ref, acc_ref):
    g = pl.program_id(1)

    @pl.when(g == 0)
    def _():
        t_ref[...] = _rmsnorm(x1_ref[...], fg_ref[...]).astype(BF)
        acc_ref[...] = jnp.zeros_like(acc_ref)

    t = t_ref[...]
    comb = comb_ref[...]
    lane = lax.broadcasted_iota(jnp.int32, comb.shape, 1)
    acc = acc_ref[...]
    for k in range(EPG):
        hg = jnp.dot(t, wg_ref[k], preferred_element_type=F32)
        hu = jnp.dot(t, wu_ref[k], preferred_element_type=F32)
        ce = jnp.sum(jnp.where(lane == g * EPG + k, comb, 0.0), axis=-1, keepdims=True)
        act = (hg * _sigmoid(hg)) * hu * ce
        acc = acc + jnp.dot(act.astype(BF), wd_ref[k], preferred_element_type=F32)
    acc_ref[...] = acc

    @pl.when(g == N_GROUPS - 1)
    def _():
        y_ref[...] = _rmsnorm(x1_ref[...] + acc, ng_ref[...])


def _moe_call(x1, comb, ffn_g, final_g, wg, wu, wd, tm):
    n = x1.shape[0]
    z2 = lambda i, e: (0, 0)
    return pl.pallas_call(
        _moe_kernel,
        grid=(n // tm, N_GROUPS),
        in_specs=[
            pl.BlockSpec((tm, D), lambda i, e: (i, 0)),
            pl.BlockSpec((tm, LANES), lambda i, e: (i, 0)),
            pl.BlockSpec((1, D), z2),
            pl.BlockSpec((1, D), z2),
            pl.BlockSpec((EPG, D, D_EXPERT), lambda i, e: (e, 0, 0)),
            pl.BlockSpec((EPG, D, D_EXPERT), lambda i, e: (e, 0, 0)),
            pl.BlockSpec((EPG, D_EXPERT, D), lambda i, e: (e, 0, 0)),
        ],
        out_specs=pl.BlockSpec((tm, D), lambda i, e: (i, 0)),
        out_shape=jax.ShapeDtypeStruct((n, D), F32),
        scratch_shapes=[pltpu.VMEM((tm, D), BF), pltpu.VMEM((tm, D), F32)],
        compiler_params=pltpu.CompilerParams(
            dimension_semantics=("arbitrary", "arbitrary"), vmem_limit_bytes=VMEM_LIMIT),
        name="moe",
    )(x1, comb, ffn_g, final_g, wg, wu, wd)


def _moe_sparse_kernel(n_seq_steps, slot_ref, e_lo_ref, e_hi_ref, nvalid_ref, nused_ref,
                       rows_hbm, fg_ref, ng_ref, wg_lo, wu_lo, wd_lo, wg_hi, wu_hi, wd_hi,
                       dproj_ref, ds_ref, dsc_ref, dcos_ref, dsin_ref, dgam_ref, dcw_ref, dcb_ref,
                       y_hbm, do_ref, dc_ref, ds_out_ref, dsc_out_ref,
                       inv_ref, xbuf, ybuf, gsem, ssem):
    t = pl.program_id(0)
    n_used = nused_ref[0]
    n_tok = slot_ref.shape[0]
    cur = lax.rem(t, 2)

    @pl.when(t < n_seq_steps)
    def _():
        _sample_seq_step(t, dproj_ref, ds_ref, dsc_ref, dcos_ref, dsin_ref, dgam_ref, dcw_ref, dcb_ref,
                         do_ref, dc_ref, ds_out_ref, dsc_out_ref)

    def hbm_row(ref, tok):
        return ref.at[jnp.right_shift(tok, 3), pl.ds(tok & (SUBLANES - 1), 1), :]

    def gather_copy(grp, sub, tok, buf):
        return pltpu.make_async_copy(hbm_row(rows_hbm, tok), xbuf.at[buf, grp, pl.ds(sub, 1), :],
                                     gsem.at[buf])

    def scatter_copy(grp, sub, tok, buf):
        return pltpu.make_async_copy(ybuf.at[buf, grp, pl.ds(sub, 1), :], hbm_row(y_hbm, tok),
                                     ssem.at[buf])

    def start_rows(copy, tile, buf):
        base = tile * MOE_TM
        n_rows = nvalid_ref[tile]
        n_grp = n_rows // SUBLANES

        def group(g, c):
            for u in range(SUBLANES):
                copy(g, u, inv_ref[base + g * SUBLANES + u], buf).start()
            return c
        lax.fori_loop(0, n_grp, group, 0)

        def tail(r, c):
            copy(n_grp, r - n_grp * SUBLANES, inv_ref[base + r], buf).start()
            return c
        lax.fori_loop(n_grp * SUBLANES, n_rows, tail, 0)

    def wait_rows(copy, tile, buf):
        n_rows = nvalid_ref[tile]
        vbuf, hbm, sem = (xbuf, rows_hbm, gsem) if copy is gather_copy else (ybuf, y_hbm, ssem)
        k = MOE_TM
        while k >= 1:
            @pl.when((n_rows & k) != 0)
            def _(k=k):
                if k >= SUBLANES:
                    vmem = vbuf.at[buf, pl.ds(0, k // SUBLANES)]
                    ext = hbm.at[pl.ds(0, k // SUBLANES)]
                else:
                    vmem = vbuf.at[buf, 0, pl.ds(0, k), :]
                    ext = hbm.at[0, pl.ds(0, k), :]
                pair = (ext, vmem) if copy is gather_copy else (vmem, ext)
                pltpu.make_async_copy(pair[0], pair[1], sem.at[buf]).wait()
            k //= 2

    @pl.when(t == 0)
    def _():
        def fill(tok, c):
            inv_ref[slot_ref[tok]] = tok
            return c
        lax.fori_loop(0, n_tok, fill, 0, unroll=FILL_UNROLL)
        xbuf[...] = jnp.zeros_like(xbuf)
        start_rows(gather_copy, 0, 0)

    @pl.when(t < n_used)
    def _():
        wait_rows(gather_copy, t, cur)

        @pl.when(t + 1 < n_used)
        def _():
            start_rows(gather_copy, t + 1, 1 - cur)

        xt = xbuf[cur].reshape(MOE_TM, ROW_W)
        x1 = xt[:, 0:D]
        aux = xt[:, D:ROW_W]
        tb = _rmsnorm(x1, fg_ref[...]).astype(BF)
        acc = None
        for wg, wu, wd, lane_w in ((wg_lo, wu_lo, wd_lo, AUX_W_LO), (wg_hi, wu_hi, wd_hi, AUX_W_HI)):
            hg = jnp.dot(tb, wg[0], preferred_element_type=F32)
            hu = jnp.dot(tb, wu[0], preferred_element_type=F32)
            act = (hg * _sigmoid(hg)) * hu * aux[:, lane_w:lane_w + 1]
            part = jnp.dot(act.astype(BF), wd[0], preferred_element_type=F32)
            acc = part if acc is None else acc + part
        y = _rmsnorm(x1 + acc, ng_ref[...])

        @pl.when(t >= 2)
        def _():
            wait_rows(scatter_copy, jnp.maximum(t - 2, 0), cur)
        ybuf[cur] = y.reshape(MOE_TM // SUBLANES, SUBLANES, D)
        start_rows(scatter_copy, t, cur)

        @pl.when(t == n_used - 1)
        def _():
            wait_rows(scatter_copy, t, cur)

            @pl.when(t >= 1)
            def _():
                wait_rows(scatter_copy, jnp.maximum(t - 1, 0), 1 - cur)


def _bin_tables():
    lo, hi = [], []
    for g in range(N_GROUPS):
        for a in range(EPG):
            for b in range(a + 1, EPG):
                lo.append(g * EPG + a)
                hi.append(g * EPG + b)
    return jnp.array(lo, jnp.int32), jnp.array(hi, jnp.int32)


def _moe_sparse_call(rows, route, counts, ffn_g, final_g, wg, wu, wd, decode):
    proj_s, state_ret, state_conv, cos_s, sin_s, gam, cw, cb = decode
    n_seq = proj_s.shape[0]
    n_seq_steps = n_seq // SEQ_BLK
    n = route.shape[1]
    max_tiles = n // MOE_TM + N_BINS
    assert max_tiles >= n_seq_steps
    padded = ((counts + MOE_TM - 1) // MOE_TM) * MOE_TM
    ends = jnp.cumsum(padded)
    offs = ends - padded
    n_used = ends[-1] // MOE_TM
    slot = offs[route[ROUTE_BIN]] + route[ROUTE_RANK]
    tile_start = jnp.minimum(jnp.arange(max_tiles, dtype=jnp.int32), n_used - 1) * MOE_TM
    tile_bin = jnp.sum((tile_start[:, None] >= ends[None, :]).astype(jnp.int32), axis=1)
    bin_lo, bin_hi = _bin_tables()
    e_lo = bin_lo[tile_bin]
    e_hi = bin_hi[tile_bin]
    tile_id = jnp.arange(max_tiles, dtype=jnp.int32)
    n_valid = jnp.where(tile_id < n_used,
                        jnp.clip(counts[tile_bin] - (tile_start - offs[tile_bin]), 0, MOE_TM), 0)

    z2 = lambda t, *_: (0, 0)
    w_lo_map = lambda t, slot_r, lo_r, hi_r, nv_r, nu_r: (lo_r[t], 0, 0)
    w_hi_map = lambda t, slot_r, lo_r, hi_r, nv_r, nu_r: (hi_r[t], 0, 0)
    seq3 = lambda t, *_: (jnp.minimum(t, n_seq_steps - 1), 0, 0)
    seq4 = lambda t, *_: (jnp.minimum(t, n_seq_steps - 1), 0, 0, 0)
    y, o, c, s_new, sc_new = pl.pallas_call(
        functools.partial(_moe_sparse_kernel, n_seq_steps),
        grid_spec=pltpu.PrefetchScalarGridSpec(
            num_scalar_prefetch=5,
            grid=(max_tiles,),
            in_specs=[
                pl.BlockSpec(memory_space=pl.ANY),
                pl.BlockSpec((1, D), z2),
                pl.BlockSpec((1, D), z2),
                pl.BlockSpec((1, D, D_EXPERT), w_lo_map),
                pl.BlockSpec((1, D, D_EXPERT), w_lo_map),
                pl.BlockSpec((1, D_EXPERT, D), w_lo_map),
                pl.BlockSpec((1, D, D_EXPERT), w_hi_map),
                pl.BlockSpec((1, D, D_EXPERT), w_hi_map),
                pl.BlockSpec((1, D_EXPERT, D), w_hi_map),
                pl.BlockSpec((n_seq, D_IN), z2),
                pl.BlockSpec((SEQ_BLK, N_HEADS, DK, DV), seq4),
                pl.BlockSpec((SEQ_BLK, CONV_W - 1, D), seq3),
                pl.BlockSpec((1, LANES), z2),
                pl.BlockSpec((1, LANES), z2),
                pl.BlockSpec(memory_space=pltpu.SMEM),
                pl.BlockSpec((CONV_W, D), z2),
                pl.BlockSpec((1, D), z2),
            ],
            out_specs=[
                pl.BlockSpec(memory_space=pl.ANY),
                pl.BlockSpec((n_seq, D_V), z2),
                pl.BlockSpec((n_seq, D), z2),
                pl.BlockSpec((SEQ_BLK, N_HEADS, DK, DV), seq4),
                pl.BlockSpec((SEQ_BLK, CONV_W - 1, D), seq3),
            ],
            scratch_shapes=[
                pltpu.SMEM((max_tiles * MOE_TM,), jnp.int32),
                pltpu.VMEM((2, MOE_TM // SUBLANES, SUBLANES, ROW_W), F32),
                pltpu.VMEM((2, MOE_TM // SUBLANES, SUBLANES, D), F32),
                pltpu.SemaphoreType.DMA((2,)),
                pltpu.SemaphoreType.DMA((2,)),
            ],
        ),
        out_shape=[
            jax.ShapeDtypeStruct((n // SUBLANES, SUBLANES, D), F32),
            jax.ShapeDtypeStruct((n_seq, D_V), F32),
            jax.ShapeDtypeStruct((n_seq, D), F32),
            jax.ShapeDtypeStruct(state_ret.shape, F32),
            jax.ShapeDtypeStruct(state_conv.shape, F32),
        ],
        compiler_params=pltpu.CompilerParams(
            dimension_semantics=("arbitrary",), vmem_limit_bytes=VMEM_LIMIT),
        name="moe_sparse",
    )(slot, e_lo, e_hi, n_valid, n_used.reshape(1).astype(jnp.int32),
      rows.reshape(n // SUBLANES, SUBLANES, ROW_W), ffn_g, final_g, wg, wu, wd, wg, wu, wd,
      proj_s, state_ret, state_conv, cos_s, sin_s, gam, cw, cb)
    return y.reshape(n, D), o, c, s_new, sc_new


def _rope_tables(pos):
    half = DK // 2
    freqs = np.power(np.float64(ROPE_BASE), -np.arange(half, dtype=np.float64) / half)
    ang = pos.astype(np.float64)[:, None] * freqs[None, :]
    return jnp.asarray(np.cos(ang), F32), jnp.asarray(np.sin(ang), F32)


def _decay_tables(L):
    lg = np.log(1.0 - np.exp2(-5.0 - np.arange(N_HEADS, dtype=np.float64)))
    idx = np.arange(L, dtype=np.float64)
    diff = idx[:, None] - idx[None, :]
    dmat = np.where((diff >= 0)[None], np.exp(np.maximum(diff, 0.0)[None] * lg[:, None, None]), 0.0)
    dq = np.exp((idx[:, None] + 1.0) * lg[None, :]).T
    dk = np.exp((L - 1.0 - idx)[:, None] * lg[None, :]).T
    gl = np.exp(L * lg)
    bl = lambda a: jnp.asarray(np.broadcast_to(a[:, :, None], (N_HEADS, L, LANES)), F32)
    return jnp.asarray(dmat, F32), bl(dq), bl(dk), jnp.asarray(gl, F32)


def kernel(x_prompt, x_sample, state_ret, state_conv, meta_tokens, norm_mix_g, w_in, b_gates, conv_w, conv_b, conv_ln_g, conv_ln_b, w_conv_out, ret_gn_g, w_ret_out, w_o, norm_ffn_g, w_coarse, b_coarse, w_fine, b_fine, w_gate_e, w_up_e, w_down_e, norm_final_g):
    bp, seq, _ = x_prompt.shape
    ns = x_sample.shape[0]
    l = 0
    row = lambda a: a.reshape(1, -1)

    w_in_bf = w_in[l].astype(BF)
    wr = jnp.zeros((D, LANES), F32).at[:, :N_EXPERTS].set(w_fine[l]).at[:, N_EXPERTS:N_EXPERTS + N_GROUPS].set(w_coarse[l])
    wr_hi = wr.astype(BF)
    wr_lo = (wr - wr_hi.astype(F32)).astype(BF)
    br = jnp.zeros((1, LANES), F32).at[0, :N_EXPERTS].set(b_fine[l]).at[0, N_EXPERTS:N_EXPERTS + N_GROUPS].set(b_coarse[l])
    post_args = (row(conv_ln_g[l]), row(conv_ln_b[l]), w_conv_out[l].astype(BF), row(ret_gn_g[l]),
                 w_ret_out[l].astype(BF), row(b_gates[l]), w_o[l].astype(BF), row(norm_ffn_g[l]),
                 wr_hi, wr_lo, br)
    wg = w_gate_e[l].astype(BF)
    wu = w_up_e[l].astype(BF)
    wd = w_down_e[l].astype(BF)
    mix_g = row(norm_mix_g[l])
    cw = conv_w[l]
    cb = row(conv_b[l])

    small = dict(zip(_POST_NAMES, post_args))
    small_args = tuple(small[k] for k in _SMALL_NAMES)
    big_args = (w_in_bf,) + tuple(small[k] for k in _BIG_NAMES[1:])
    r = np.arange(PT)
    tri = jnp.asarray(r[None, :] < r[:, None], BF)
    consts = (cw, cb, tri, mix_g)
    decay = _decay_tables(PT)
    x_meta = jnp.concatenate([jnp.zeros((PT - N_META, D), F32), meta_tokens])[None]
    pos_meta = np.maximum(np.arange(PT) - (PT - N_META), 0)
    _, _, _, s_meta, buf_meta = _prompt_call(
        x_meta, jnp.zeros((N_HEADS, DK, DV), F32), jnp.zeros((HALO, D), F32),
        _rope_tables(pos_meta) + decay, consts, small_args, big_args, "meta_state")
    pos_p = N_META + np.arange(seq)
    rows_p, route_p, counts, s_new_p, buf_p = _prompt_call(
        x_prompt, s_meta[0], buf_meta[0], _rope_tables(pos_p) + decay, consts, small_args, big_args,
        "prompt_mix")

    xs = x_sample.reshape(ns, D)
    proj_s = _proj_call(xs, mix_g, w_in_bf, tm=ns)
    pos_s = np.full((1,), PAST_LEN)
    cos_s, sin_s = _rope_tables(pos_s)
    _, _, _, gam = _decay_tables(1)

    y_p, o_s, c_s, s_new_s, buf_s = _moe_sparse_call(
        rows_p.reshape(bp * seq, ROW_W), route_p,
        counts[0, :N_BINS].astype(jnp.int32), row(norm_ffn_g[l]), row(norm_final_g), wg, wu, wd,
        (proj_s, state_ret[l], state_conv[l], cos_s, sin_s, gam, cw, cb))

    x1_s, comb_s = _post_call(c_s, o_s, proj_s, xs, post_args)
    y_s = _moe_call(x1_s, comb_s, row(norm_ffn_g[l]), row(norm_final_g), wg, wu, wd, tm=ns).reshape(ns, 1, D)

    return (y_p.reshape(bp, seq, D), y_s, s_new_p[None], buf_p[:, HALO - (CONV_W - 1):][None],
            s_new_s[None], buf_s[None])
```

```python
import functools

import jax
import jax.numpy as jnp
import numpy as np
from jax import lax
from jax.experimental import pallas as pl
from jax.experimental.pallas import tpu as pltpu

D = 1024
N_META = 16
PAST_LEN = 16384
CONV_W = 31
N_HEADS = 4
DK = 256
DV = 512
D_QK = N_HEADS * DK
D_V = N_HEADS * DV
ROPE_BASE = 10000.0
N_GROUPS = 4
EPG = 4
N_EXPERTS = 16
D_EXPERT = 512
EPS = 1e-6
D_IN = 2 * D + 2 * D_QK + 2 * D_V + 2 * D

O_GLU_A, O_GLU_B = 0, D
O_Q, O_K = 2 * D, 2 * D + D_QK
O_V = 2 * D + 2 * D_QK
O_GRET = O_V + D_V
O_GA = O_GRET + D_V
O_GB = O_GA + D

N_PAIRS = EPG * (EPG - 1) // 2
N_BINS = N_GROUPS * N_PAIRS

LANES = 128
SUBLANES = 8
MXU_DIM = 256
ROW_W = D + LANES
AUX_W_LO, AUX_W_HI = 0, 1
ROUTE_BIN, ROUTE_RANK = 0, 1
MOE_TM = MXU_DIM
FILL_UNROLL = 8
PT = MXU_DIM
CONV_COLS = MXU_DIM
HALO = 32
VMEM_LIMIT = 56 * 1024 * 1024

BF = jnp.bfloat16
F32 = jnp.float32


def _sigmoid(x):
    return 1.0 / (1.0 + jnp.exp(-x))


def _bdot(a, b):
    return jnp.dot(a.astype(BF), b.astype(BF), preferred_element_type=F32)


def _rmsnorm(x, g):
    return x * lax.rsqrt(jnp.mean(x * x, axis=-1, keepdims=True) + EPS) * g


def _proj_kernel(x_ref, g_ref, w_ref, o_ref):
    h = _rmsnorm(x_ref[...], g_ref[...])
    o_ref[...] = jnp.dot(h.astype(BF), w_ref[...], preferred_element_type=F32)


def _proj_call(x2d, g, w_bf, tm, tn=2048):
    n = x2d.shape[0]
    return pl.pallas_call(
        _proj_kernel,
        grid=(D_IN // tn, n // tm),
        in_specs=[
            pl.BlockSpec((tm, D), lambda j, i: (i, 0)),
            pl.BlockSpec((1, D), lambda j, i: (0, 0)),
            pl.BlockSpec((D, tn), lambda j, i: (0, j)),
        ],
        out_specs=pl.BlockSpec((tm, tn), lambda j, i: (i, j)),
        out_shape=jax.ShapeDtypeStruct((n, D_IN), F32),
        compiler_params=pltpu.CompilerParams(
            dimension_semantics=("arbitrary", "arbitrary"), vmem_limit_bytes=VMEM_LIMIT),
        name="in_proj",
    )(x2d, g, w_bf)


def _route(t, wr_hi_ref, wr_lo_ref, br_ref):
    t_hi = t.astype(BF)
    t_lo = (t - t_hi.astype(F32)).astype(BF)
    wh = wr_hi_ref[...]
    logits = (jnp.dot(t_hi, wh, preferred_element_type=F32)
              + jnp.dot(t_hi, wr_lo_ref[...], preferred_element_type=F32)
              + jnp.dot(t_lo, wh, preferred_element_type=F32)) + br_ref[...]
    lt = logits.T
    row = lambda r: lt[r:r + 1, :]
    neg = jnp.float32(-jnp.inf)
    cmax, g_sel = _first_max([row(N_EXPERTS + k) for k in range(N_GROUPS)])
    csum = sum(jnp.exp(row(N_EXPERTS + k) - cmax) for k in range(N_GROUPS))
    p_g = 1.0 / csum
    fine = []
    for k in range(EPG):
        cand = [row(grp * EPG + k) for grp in range(N_GROUPS)]
        fine.append(jnp.where(g_sel == 0, cand[0], jnp.where(g_sel == 1, cand[1],
                                                             jnp.where(g_sel == 2, cand[2], cand[3]))))
    v1, j1 = _first_max(fine)
    v2, j2 = _first_max([jnp.where(j1 == k, neg, fine[k]) for k in range(EPG)])
    e2 = jnp.exp(v2 - v1)
    w1 = p_g / (1.0 + e2)
    w2 = e2 * w1
    first_low = j1 < j2
    a = jnp.where(first_low, j1, j2)
    b = jnp.where(first_low, j2, j1)
    pair = jnp.where(a == 0, 0, jnp.where(a == 1, 3, 5)) + (b - a - 1)
    bin_id = g_sel * N_PAIRS + pair
    w_lo = jnp.where(first_low, w1, w2)
    w_hi = jnp.where(first_low, w2, w1)
    sub = lax.broadcasted_iota(jnp.int32, lt.shape, 0)
    comb = jnp.where(sub == g_sel * EPG + j1, w1, jnp.where(sub == g_sel * EPG + j2, w2, 0.0)).T
    rec = jnp.where(sub == 0, w_lo, jnp.where(sub == 1, w_hi,
                                              jnp.where(sub == 2, bin_id.astype(F32), 0.0))).T
    return comb, (rec[:, 0:1], rec[:, 1:2], rec[:, 2:3].astype(jnp.int32))


def _first_max(vals):
    b01 = vals[1] > vals[0]
    b23 = vals[3] > vals[2]
    m01 = jnp.where(b01, vals[1], vals[0])
    m23 = jnp.where(b23, vals[3], vals[2])
    top = m23 > m01
    idx = jnp.where(top, jnp.where(b23, 3, 2), jnp.where(b01, 1, 0)).astype(jnp.int32)
    return jnp.where(top, m23, m01), idx


def _silu(x):
    return x * _sigmoid(x)


def _post_tokens(c, o, gret_act, g_a, g_b, x, p):
    mu = jnp.mean(c, axis=-1, keepdims=True)
    dc = c - mu
    var = jnp.mean(dc * dc, axis=-1, keepdims=True)
    cn = dc * lax.rsqrt(var + EPS) * p["ln_g"][...] + p["ln_b"][...]
    y_a = _bdot(_silu(cn), p["w_conv_out"][...])
    parts = []
    for h in range(N_HEADS):
        oh = o[:, h * DV:(h + 1) * DV]
        omu = jnp.mean(oh, axis=-1, keepdims=True)
        od = oh - omu
        ovar = jnp.mean(od * od, axis=-1, keepdims=True)
        parts.append(od * lax.rsqrt(ovar + EPS))
    on = jnp.concatenate(parts, axis=-1) * p["gn_g"][...]
    y_b = _bdot(on * gret_act, p["w_ret_out"][...])
    x1 = x + _bdot(g_a * y_a + g_b * y_b, p["w_o"][...])
    t = _rmsnorm(x1, p["ffn_g"][...])
    comb, pair_route = _route(t, p["wr_hi"], p["wr_lo"], p["br"])
    return x1, comb, pair_route


_POST_NAMES = ("ln_g", "ln_b", "w_conv_out", "gn_g", "w_ret_out", "b_gates", "w_o", "ffn_g",
               "wr_hi", "wr_lo", "br")


def _post_specs():
    z2 = lambda *_: (0, 0)
    shapes = {"ln_g": (1, D), "ln_b": (1, D), "w_conv_out": (D, D), "gn_g": (1, D_V),
              "w_ret_out": (D_V, D), "b_gates": (1, 2 * D), "w_o": (D, D), "ffn_g": (1, D),
              "wr_hi": (D, LANES), "wr_lo": (D, LANES), "br": (1, LANES)}
    return [pl.BlockSpec(shapes[k], z2) for k in _POST_NAMES]


def _rotary(x, cos, sin):
    x1, x2 = x[:, :LANES], x[:, LANES:]
    return jnp.concatenate([x1 * cos - x2 * sin, x1 * sin + x2 * cos], axis=-1)


_SMALL_NAMES = ("ln_g", "ln_b", "gn_g", "b_gates", "ffn_g", "wr_hi", "wr_lo", "br")
_BIG_NAMES = ("w_in", "w_conv_out", "w_ret_out", "w_o")


def _prompt_kernel(x_ref, buf0_ref, cos_ref, sin_ref, dmat_ref, dq_ref, dk_ref, gl_ref,
                   cw_ref, cb_ref, tri_ref, mixg_ref, *rest):
    ns, nb = len(_SMALL_NAMES), len(_BIG_NAMES)
    small = dict(zip(_SMALL_NAMES, rest[:ns]))
    s0_hbm = rest[ns]
    big_hbm = rest[ns + 1:ns + 1 + nb]
    x1e_ref, route_ref, cnt_out_ref, s_out_hbm, buf_out_ref = rest[ns + 1 + nb:ns + 6 + nb]
    scratch = rest[ns + 6 + nb:]
    big = dict(zip(_BIG_NAMES, scratch[:nb]))
    wsem, ssem, shift_ref, tail_ref, s_ref, c_ref, o_ref, h_ref, cnt_ref = scratch[nb:]
    p = dict(small, **{k: big[k] for k in _BIG_NAMES[1:]})
    w_in = big["w_in"]
    b = pl.program_id(0)
    i = pl.program_id(1)

    @pl.when((b == 0) & (i == 0))
    def _():
        copies = [pltpu.make_async_copy(src, big[k], wsem.at[n])
                  for n, (k, src) in enumerate(zip(_BIG_NAMES, big_hbm))]
        for cp in copies:
            cp.start()
        for cp in copies:
            cp.wait()
        cnt_ref[...] = jnp.zeros_like(cnt_ref)

    @pl.when(i == 0)
    def _():
        load_state = pltpu.make_async_copy(s0_hbm, s_ref, ssem)
        load_state.start()
        load_state.wait()
        tail_ref[...] = buf0_ref[...]

    h_ref[...] = _rmsnorm(x_ref[0], mixg_ref[...]).astype(BF)

    def proj(lo, width):
        return jnp.dot(h_ref[...], w_in[:, lo:lo + width], preferred_element_type=F32)

    base = HALO - (CONV_W - 1)

    def conv_block(cb):
        c0 = cb * CONV_COLS
        par = cb % 2
        u = proj(O_GLU_A + c0, CONV_COLS) * _sigmoid(proj(O_GLU_B + c0, CONV_COLS))
        for half in range(CONV_COLS // LANES):
            cols = slice(c0 + half * LANES, c0 + (half + 1) * LANES)
            uh = u[:, half * LANES:(half + 1) * LANES]
            for s in range(SUBLANES):
                shift_ref[par, s, half, 0:HALO - s, :] = tail_ref[s:HALO, cols]
                shift_ref[par, s, half, HALO - s:HALO - s + PT, :] = uh
            tail_ref[:, cols] = uh[PT - HALO:PT, :]
            acc = jnp.broadcast_to(cb_ref[:, cols], (PT, LANES))
            for j in range(CONV_W):
                s = (base + j) % SUBLANES
                start = base + j - s
                acc = acc + cw_ref[j:j + 1, cols] * shift_ref[par, s, half, start:start + PT, :]
            c_ref[:, cols] = acc

    cos = cos_ref[...]
    sin = sin_ref[...]

    def retention_head(h):
        q = _rotary(proj(O_Q + h * DK, DK), cos, sin)
        k = _rotary(proj(O_K + h * DK, DK), cos, sin) * (DK ** -0.5)
        v = proj(O_V + h * DV, DV).astype(BF)
        dq = dq_ref[h]
        dk = dk_ref[h]
        scores = lax.dot_general(q.astype(BF), k.astype(BF), (((1,), (1,)), ((), ())),
                                 preferred_element_type=F32) * dmat_ref[h]
        s_old = s_ref[h]
        q_dec = jnp.concatenate([q[:, :LANES] * dq, q[:, LANES:] * dq], axis=-1)
        k_dec = jnp.concatenate([k[:, :LANES] * dk, k[:, LANES:] * dk], axis=-1)
        o_ref[:, h * DV:(h + 1) * DV] = (jnp.dot(scores.astype(BF), v, preferred_element_type=F32)
                                         + _bdot(q_dec, s_old))
        s_ref[h] = gl_ref[h] * s_old + lax.dot_general(
            k_dec.astype(BF), v, (((0,), (0,)), ((), ())), preferred_element_type=F32)

    assert D // CONV_COLS == N_HEADS
    for n in range(N_HEADS):
        conv_block(n)
        retention_head(n)

    @pl.when(i == pl.num_programs(1) - 1)
    def _():
        store_state = pltpu.make_async_copy(s_ref, s_out_hbm.at[b], ssem)
        store_state.start()
        store_state.wait()
        buf_out_ref[0] = tail_ref[...]

    bg = small["b_gates"]
    x1, _, (w_lo, w_hi, bin_id) = _post_tokens(c_ref[...], o_ref[...], _silu(proj(O_GRET, D_V)),
                                               _sigmoid(proj(O_GA, D) + bg[:, :D]),
                                               _sigmoid(proj(O_GB, D) + bg[:, D:]), x_ref[0], p)

    lane = lax.broadcasted_iota(jnp.int32, (PT, LANES), 1)
    onehot = lane == bin_id
    earlier = jnp.dot(tri_ref[...], onehot.astype(BF), preferred_element_type=F32)
    rank = jnp.sum(jnp.where(onehot, earlier + cnt_ref[...], 0.0), axis=-1, keepdims=True)
    cnt_ref[...] += jnp.sum(onehot.astype(F32), axis=0, keepdims=True)

    x1e_ref[0, :, 0:D] = x1
    x1e_ref[0, :, D:ROW_W] = jnp.where(lane == AUX_W_LO, w_lo, jnp.where(lane == AUX_W_HI, w_hi, 0.0))
    record = jnp.where(lane == ROUTE_BIN, bin_id, jnp.where(lane == ROUTE_RANK, rank.astype(jnp.int32), 0))
    route_ref[...] = record.T[0:SUBLANES, :]
    cnt_out_ref[...] = cnt_ref[...]


def _prompt_call(x, s0, buf0, tabs, consts, small, big, name):
    b, t, _ = x.shape
    cos, sin, dmat, dq, dk, gl = tabs
    z2 = lambda bi, i: (0, 0)
    z3 = lambda bi, i: (0, 0, 0)
    tok = lambda bi, i: (bi, i, 0)
    whole = lambda a: pl.BlockSpec(a.shape, z2 if a.ndim == 2 else z3)
    any_spec = pl.BlockSpec(memory_space=pl.ANY)
    return pl.pallas_call(
        _prompt_kernel,
        grid=(b, t // PT),
        in_specs=[
            pl.BlockSpec((1, PT, D), tok),
            whole(buf0),
            pl.BlockSpec((PT, LANES), lambda bi, i: (i, 0)),
            pl.BlockSpec((PT, LANES), lambda bi, i: (i, 0)),
            whole(dmat), whole(dq), whole(dk),
            pl.BlockSpec(memory_space=pltpu.SMEM),
        ] + [whole(a) for a in consts] + [whole(a) for a in small] + [any_spec] * (1 + len(big)),
        out_specs=[
            pl.BlockSpec((1, PT, ROW_W), tok),
            pl.BlockSpec((SUBLANES, PT), lambda bi, i: (0, bi * (t // PT) + i)),
            pl.BlockSpec((1, LANES), z2),
            any_spec,
            pl.BlockSpec((1, HALO, D), lambda bi, i: (bi, 0, 0)),
        ],
        out_shape=[
            jax.ShapeDtypeStruct((b, t, ROW_W), F32),
            jax.ShapeDtypeStruct((SUBLANES, b * t), jnp.int32),
            jax.ShapeDtypeStruct((1, LANES), F32),
            jax.ShapeDtypeStruct((b, N_HEADS, DK, DV), F32),
            jax.ShapeDtypeStruct((b, HALO, D), F32),
        ],
        scratch_shapes=[pltpu.VMEM(w.shape, BF) for w in big] + [
            pltpu.SemaphoreType.DMA((len(big),)),
            pltpu.SemaphoreType.DMA,
            pltpu.VMEM((2, SUBLANES, CONV_COLS // LANES, HALO + PT, LANES), F32),
            pltpu.VMEM((HALO, D), F32),
            pltpu.VMEM((N_HEADS, DK, DV), F32),
            pltpu.VMEM((PT, D), F32),
            pltpu.VMEM((PT, D_V), F32),
            pltpu.VMEM((PT, D), BF),
            pltpu.VMEM((1, LANES), F32),
        ],
        compiler_params=pltpu.CompilerParams(
            dimension_semantics=("arbitrary", "arbitrary"), vmem_limit_bytes=VMEM_LIMIT),
        name=name,
    )(x, buf0, cos, sin, dmat, dq, dk, gl, *consts, *small, s0, *big)


SEQ_BLK = 2
MXU_ROWS = 16


def _sample_seq_step(step, proj_ref, s_ref, sc_ref, cos_ref, sin_ref, gam_ref, cw_ref, cb_ref,
                     o_ref, c_ref, s_out_ref, sc_out_ref):
    cos = cos_ref[...]
    sin = sin_ref[...]
    row = lax.broadcasted_iota(jnp.int32, (MXU_ROWS, DK), 0)
    for s in range(SEQ_BLK):
        seq = pl.ds(step * SEQ_BLK + s, 1)
        us = proj_ref[seq, O_GLU_A:O_GLU_A + D] * _sigmoid(proj_ref[seq, O_GLU_B:O_GLU_B + D])
        c_ref[seq, :] = (
            jnp.sum(cw_ref[0:CONV_W - 1, :] * sc_ref[s], axis=0, keepdims=True)
            + cw_ref[CONV_W - 1:CONV_W, :] * us + cb_ref[...])
        sc_out_ref[s, 0:CONV_W - 2, :] = sc_ref[s, 1:CONV_W - 1, :]
        sc_out_ref[s, CONV_W - 2:CONV_W - 1, :] = us
        for h in range(N_HEADS):
            q = _rotary(proj_ref[seq, O_Q + h * DK:O_Q + (h + 1) * DK], cos, sin)
            k = _rotary(proj_ref[seq, O_K + h * DK:O_K + (h + 1) * DK], cos, sin) * (DK ** -0.5)
            v = proj_ref[seq, O_V + h * DV:O_V + (h + 1) * DV]
            gam = gam_ref[h]
            qk = jnp.sum(q * k, axis=-1, keepdims=True)
            s_old = s_ref[s, h]
            o8 = _bdot(jnp.broadcast_to(q * gam, (MXU_ROWS, DK)), s_old)
            o_ref[seq, h * DV:(h + 1) * DV] = qk * v + o8[0:1, :]
            k8 = jnp.where(row == 0, jnp.broadcast_to(k, (MXU_ROWS, DK)), 0.0).astype(BF)
            v8 = jnp.broadcast_to(v, (MXU_ROWS, DV)).astype(BF)
            kv = lax.dot_general(k8, v8, (((0,), (0,)), ((), ())), preferred_element_type=F32)
            s_out_ref[s, h] = gam * s_old + kv


def _post_kernel(c_ref, o_ref, gret_ref, gates_ref, x_ref, *rest):
    p = dict(zip(_POST_NAMES, rest[:len(_POST_NAMES)]))
    x1_ref, comb_ref = rest[len(_POST_NAMES):]
    bg = p["b_gates"][...]
    x1, comb, _ = _post_tokens(c_ref[...], o_ref[...], _silu(gret_ref[...]),
                               _sigmoid(gates_ref[:, :D] + bg[:, :D]), _sigmoid(gates_ref[:, D:] + bg[:, D:]),
                               x_ref[...], p)
    x1_ref[...] = x1
    comb_ref[...] = comb


def _post_call(c, o, proj_s, x_s, post_args):
    n = x_s.shape[0]
    z2 = lambda i: (0, 0)
    return pl.pallas_call(
        _post_kernel,
        grid=(1,),
        in_specs=[
            pl.BlockSpec((n, D), z2),
            pl.BlockSpec((n, D_V), z2),
            pl.BlockSpec((n, D_V), lambda i: (0, O_GRET // D_V)),
            pl.BlockSpec((n, 2 * D), lambda i: (0, O_GA // (2 * D))),
            pl.BlockSpec((n, D), z2),
        ] + _post_specs(),
        out_specs=[pl.BlockSpec((n, D), z2), pl.BlockSpec((n, LANES), z2)],
        out_shape=[jax.ShapeDtypeStruct((n, D), F32), jax.ShapeDtypeStruct((n, LANES), F32)],
        compiler_params=pltpu.CompilerParams(
            dimension_semantics=("arbitrary",), vmem_limit_bytes=VMEM_LIMIT),
        name="sample_post",
    )(c, o, proj_s, proj_s, x_s, *post_args)


def _moe_kernel(x1_ref, comb_ref, fg_ref, ng_ref, wg_ref, wu_ref, wd_ref, y_ref, t_ref, acc_ref):
    g = pl.program_id(1)

    @pl.when(g == 0)
    def _():
        t_ref[...] = _rmsnorm(x1_ref[...], fg_ref[...]).astype(BF)
        acc_ref[...] = jnp.zeros_like(acc_ref)

    t = t_ref[...]
    comb = comb_ref[...]
    lane = lax.broadcasted_iota(jnp.int32, comb.shape, 1)
    acc = acc_ref[...]
    for k in range(EPG):
        hg = jnp.dot(t, wg_ref[k], preferred_element_type=F32)
        hu = jnp.dot(t, wu_ref[k], preferred_element_type=F32)
        ce = jnp.sum(jnp.where(lane == g * EPG + k, comb, 0.0), axis=-1, keepdims=True)
        act = (hg * _sigmoid(hg)) * hu * ce
        acc = acc + jnp.dot(act.astype(BF), wd_ref[k], preferred_element_type=F32)
    acc_ref[...] = acc

    @pl.when(g == N_GROUPS - 1)
    def _():
        y_ref[...] = _rmsnorm(x1_ref[...] + acc, ng_ref[...])


def _moe_call(x1, comb, ffn_g, final_g, wg, wu, wd, tm):
    n = x1.shape[0]
    z2 = lambda i, e: (0, 0)
    return pl.pallas_call(
        _moe_kernel,
        grid=(n // tm, N_GROUPS),
        in_specs=[
            pl.BlockSpec((tm, D), lambda i, e: (i, 0)),
            pl.BlockSpec((tm, LANES), lambda i, e: (i, 0)),
            pl.BlockSpec((1, D), z2),
            pl.BlockSpec((1, D), z2),
            pl.BlockSpec((EPG, D, D_EXPERT), lambda i, e: (e, 0, 0)),
            pl.BlockSpec((EPG, D, D_EXPERT), lambda i, e: (e, 0, 0)),
            pl.BlockSpec((EPG, D_EXPERT, D), lambda i, e: (e, 0, 0)),
        ],
        out_specs=pl.BlockSpec((tm, D), lambda i, e: (i, 0)),
        out_shape=jax.ShapeDtypeStruct((n, D), F32),
        scratch_shapes=[pltpu.VMEM((tm, D), BF), pltpu.VMEM((tm, D), F32)],
        compiler_params=pltpu.CompilerParams(
            dimension_semantics=("arbitrary", "arbitrary"), vmem_limit_bytes=VMEM_LIMIT),
        name="moe",
    )(x1, comb, ffn_g, final_g, wg, wu, wd)


def _moe_sparse_kernel(n_seq_steps, slot_ref, e_lo_ref, e_hi_ref, nvalid_ref, nused_ref,
                       rows_hbm, fg_ref, ng_ref, wg_lo, wu_lo, wd_lo, wg_hi, wu_hi, wd_hi,
                       dproj_ref, ds_ref, dsc_ref, dcos_ref, dsin_ref, dgam_ref, dcw_ref, dcb_ref,
                       y_hbm, do_ref, dc_ref, ds_out_ref, dsc_out_ref,
                       inv_ref, xbuf, ybuf, gsem, ssem):
    t = pl.program_id(0)
    n_used = nused_ref[0]
    n_tok = slot_ref.shape[0]
    cur = lax.rem(t, 2)

    @pl.when(t < n_seq_steps)
    def _():
        _sample_seq_step(t, dproj_ref, ds_ref, dsc_ref, dcos_ref, dsin_ref, dgam_ref, dcw_ref, dcb_ref,
                         do_ref, dc_ref, ds_out_ref, dsc_out_ref)

    def hbm_row(ref, tok):
        return ref.at[jnp.right_shift(tok, 3), pl.ds(tok & (SUBLANES - 1), 1), :]

    def gather_copy(grp, sub, tok, buf):
        return pltpu.make_async_copy(hbm_row(rows_hbm, tok), xbuf.at[buf, grp, pl.ds(sub, 1), :],
                                     gsem.at[buf])

    def scatter_copy(grp, sub, tok, buf):
        return pltpu.make_async_copy(ybuf.at[buf, grp, pl.ds(sub, 1), :], hbm_row(y_hbm, tok),
                                     ssem.at[buf])

    def start_rows(copy, tile, buf):
        base = tile * MOE_TM
        n_rows = nvalid_ref[tile]
        n_grp = n_rows // SUBLANES

        def group(g, c):
            for u in range(SUBLANES):
                copy(g, u, inv_ref[base + g * SUBLANES + u], buf).start()
            return c
        lax.fori_loop(0, n_grp, group, 0)

        def tail(r, c):
            copy(n_grp, r - n_grp * SUBLANES, inv_ref[base + r], buf).start()
            return c
        lax.fori_loop(n_grp * SUBLANES, n_rows, tail, 0)

    def wait_rows(copy, tile, buf):
        n_rows = nvalid_ref[tile]
        vbuf, hbm, sem = (xbuf, rows_hbm, gsem) if copy is gather_copy else (ybuf, y_hbm, ssem)
        k = MOE_TM
        while k >= 1:
            @pl.when((n_rows & k) != 0)
            def _(k=k):
                if k >= SUBLANES:
                    vmem = vbuf.at[buf, pl.ds(0, k // SUBLANES)]
                    ext = hbm.at[pl.ds(0, k // SUBLANES)]
                else:
                    vmem = vbuf.at[buf, 0, pl.ds(0, k), :]
                    ext = hbm.at[0, pl.ds(0, k), :]
                pair = (ext, vmem) if copy is gather_copy else (vmem, ext)
                pltpu.make_async_copy(pair[0], pair[1], sem.at[buf]).wait()
            k //= 2

    @pl.when(t == 0)
    def _():
        def fill(tok, c):
            inv_ref[slot_ref[tok]] = tok
            return c
        lax.fori_loop(0, n_tok, fill, 0, unroll=FILL_UNROLL)
        xbuf[...] = jnp.zeros_like(xbuf)
        start_rows(gather_copy, 0, 0)

    @pl.when(t < n_used)
    def _():
        wait_rows(gather_copy, t, cur)

        @pl.when(t + 1 < n_used)
        def _():
            start_rows(gather_copy, t + 1, 1 - cur)

        xt = xbuf[cur].reshape(MOE_TM, ROW_W)
        x1 = xt[:, 0:D]
        aux = xt[:, D:ROW_W]
        tb = _rmsnorm(x1, fg_ref[...]).astype(BF)
        acc = None
        for wg, wu, wd, lane_w in ((wg_lo, wu_lo, wd_lo, AUX_W_LO), (wg_hi, wu_hi, wd_hi, AUX_W_HI)):
            hg = jnp.dot(tb, wg[0], preferred_element_type=F32)
            hu = jnp.dot(tb, wu[0], preferred_element_type=F32)
            act = (hg * _sigmoid(hg)) * hu * aux[:, lane_w:lane_w + 1]
            part = jnp.dot(act.astype(BF), wd[0], preferred_element_type=F32)
            acc = part if acc is None else acc + part
        y = _rmsnorm(x1 + acc, ng_ref[...])

        @pl.when(t >= 2)
        def _():
            wait_rows(scatter_copy, jnp.maximum(t - 2, 0), cur)
        ybuf[cur] = y.reshape(MOE_TM // SUBLANES, SUBLANES, D)
        start_rows(scatter_copy, t, cur)

        @pl.when(t == n_used - 1)
        def _():
            wait_rows(scatter_copy, t, cur)

            @pl.when(t >= 1)
            def _():
                wait_rows(scatter_copy, jnp.maximum(t - 1, 0), 1 - cur)


def _bin_tables():
    lo, hi = [], []
    for g in range(N_GROUPS):
        for a in range(EPG):
            for b in range(a + 1, EPG):
                lo.append(g * EPG + a)
                hi.append(g * EPG + b)
    return jnp.array(lo, jnp.int32), jnp.array(hi, jnp.int32)


def _moe_sparse_call(rows, route, counts, ffn_g, final_g, wg, wu, wd, decode):
    proj_s, state_ret, state_conv, cos_s, sin_s, gam, cw, cb = decode
    n_seq = proj_s.shape[0]
    n_seq_steps = n_seq // SEQ_BLK
    n = route.shape[1]
    max_tiles = n // MOE_TM + N_BINS
    assert max_tiles >= n_seq_steps
    padded = ((counts + MOE_TM - 1) // MOE_TM) * MOE_TM
    ends = jnp.cumsum(padded)
    offs = ends - padded
    n_used = ends[-1] // MOE_TM
    slot = offs[route[ROUTE_BIN]] + route[ROUTE_RANK]
    tile_start = jnp.minimum(jnp.arange(max_tiles, dtype=jnp.int32), n_used - 1) * MOE_TM
    tile_bin = jnp.sum((tile_start[:, None] >= ends[None, :]).astype(jnp.int32), axis=1)
    bin_lo, bin_hi = _bin_tables()
    e_lo = bin_lo[tile_bin]
    e_hi = bin_hi[tile_bin]
    tile_id = jnp.arange(max_tiles, dtype=jnp.int32)
    n_valid = jnp.where(tile_id < n_used,
                        jnp.clip(counts[tile_bin] - (tile_start - offs[tile_bin]), 0, MOE_TM), 0)

    z2 = lambda t, *_: (0, 0)
    w_lo_map = lambda t, slot_r, lo_r, hi_r, nv_r, nu_r: (lo_r[t], 0, 0)
    w_hi_map = lambda t, slot_r, lo_r, hi_r, nv_r, nu_r: (hi_r[t], 0, 0)
    seq3 = lambda t, *_: (jnp.minimum(t, n_seq_steps - 1), 0, 0)
    seq4 = lambda t, *_: (jnp.minimum(t, n_seq_steps - 1), 0, 0, 0)
    y, o, c, s_new, sc_new = pl.pallas_call(
        functools.partial(_moe_sparse_kernel, n_seq_steps),
        grid_spec=pltpu.PrefetchScalarGridSpec(
            num_scalar_prefetch=5,
            grid=(max_tiles,),
            in_specs=[
                pl.BlockSpec(memory_space=pl.ANY),
                pl.BlockSpec((1, D), z2),
                pl.BlockSpec((1, D), z2),
                pl.BlockSpec((1, D, D_EXPERT), w_lo_map),
                pl.BlockSpec((1, D, D_EXPERT), w_lo_map),
                pl.BlockSpec((1, D_EXPERT, D), w_lo_map),
                pl.BlockSpec((1, D, D_EXPERT), w_hi_map),
                pl.BlockSpec((1, D, D_EXPERT), w_hi_map),
                pl.BlockSpec((1, D_EXPERT, D), w_hi_map),
                pl.BlockSpec((n_seq, D_IN), z2),
                pl.BlockSpec((SEQ_BLK, N_HEADS, DK, DV), seq4),
                pl.BlockSpec((SEQ_BLK, CONV_W - 1, D), seq3),
                pl.BlockSpec((1, LANES), z2),
                pl.BlockSpec((1, LANES), z2),
                pl.BlockSpec(memory_space=pltpu.SMEM),
                pl.BlockSpec((CONV_W, D), z2),
                pl.BlockSpec((1, D), z2),
            ],
            out_specs=[
                pl.BlockSpec(memory_space=pl.ANY),
                pl.BlockSpec((n_seq, D_V), z2),
                pl.BlockSpec((n_seq, D), z2),
                pl.BlockSpec((SEQ_BLK, N_HEADS, DK, DV), seq4),
                pl.BlockSpec((SEQ_BLK, CONV_W - 1, D), seq3),
            ],
            scratch_shapes=[
                pltpu.SMEM((max_tiles * MOE_TM,), jnp.int32),
                pltpu.VMEM((2, MOE_TM // SUBLANES, SUBLANES, ROW_W), F32),
                pltpu.VMEM((2, MOE_TM // SUBLANES, SUBLANES, D), F32),
                pltpu.SemaphoreType.DMA((2,)),
                pltpu.SemaphoreType.DMA((2,)),
            ],
        ),
        out_shape=[
            jax.ShapeDtypeStruct((n // SUBLANES, SUBLANES, D), F32),
            jax.ShapeDtypeStruct((n_seq, D_V), F32),
            jax.ShapeDtypeStruct((n_seq, D), F32),
            jax.ShapeDtypeStruct(state_ret.shape, F32),
            jax.ShapeDtypeStruct(state_conv.shape, F32),
        ],
        compiler_params=pltpu.CompilerParams(
            dimension_semantics=("arbitrary",), vmem_limit_bytes=VMEM_LIMIT),
        name="moe_sparse",
    )(slot, e_lo, e_hi, n_valid, n_used.reshape(1).astype(jnp.int32),
      rows.reshape(n // SUBLANES, SUBLANES, ROW_W), ffn_g, final_g, wg, wu, wd, wg, wu, wd,
      proj_s, state_ret, state_conv, cos_s, sin_s, gam, cw, cb)
    return y.reshape(n, D), o, c, s_new, sc_new


def _rope_tables(pos):
    half = DK // 2
    freqs = np.power(np.float64(ROPE_BASE), -np.arange(half, dtype=np.float64) / half)
    ang = pos.astype(np.float64)[:, None] * freqs[None, :]
    return jnp.asarray(np.cos(ang), F32), jnp.asarray(np.sin(ang), F32)


def _decay_tables(L):
    lg = np.log(1.0 - np.exp2(-5.0 - np.arange(N_HEADS, dtype=np.float64)))
    idx = np.arange(L, dtype=np.float64)
    diff = idx[:, None] - idx[None, :]
    dmat = np.where((diff >= 0)[None], np.exp(np.maximum(diff, 0.0)[None] * lg[:, None, None]), 0.0)
    dq = np.exp((idx[:, None] + 1.0) * lg[None, :]).T
    dk = np.exp((L - 1.0 - idx)[:, None] * lg[None, :]).T
    gl = np.exp(L * lg)
    bl = lambda a: jnp.asarray(np.broadcast_to(a[:, :, None], (N_HEADS, L, LANES)), F32)
    return jnp.asarray(dmat, F32), bl(dq), bl(dk), jnp.asarray(gl, F32)


def kernel(x_prompt, x_sample, state_ret, state_conv, meta_tokens, norm_mix_g, w_in, b_gates, conv_w, conv_b, conv_ln_g, conv_ln_b, w_conv_out, ret_gn_g, w_ret_out, w_o, norm_ffn_g, w_coarse, b_coarse, w_fine, b_fine, w_gate_e, w_up_e, w_down_e, norm_final_g):
    bp, seq, _ = x_prompt.shape
    ns = x_sample.shape[0]
    assert w_in.shape[0] == 1, "one layer"
    assert x_prompt.shape[2] == D and seq % PT == 0 and (bp * seq) % MOE_TM == 0
    assert x_sample.shape[1:] == (1, D) and ns % SEQ_BLK == 0 and ns % SUBLANES == 0
    assert meta_tokens.shape == (N_META, D) and N_META <= PT
    l = 0
    row = lambda a: a.reshape(1, -1)

    w_in_bf = w_in[l].astype(BF)
    wr = jnp.zeros((D, LANES), F32).at[:, :N_EXPERTS].set(w_fine[l]).at[:, N_EXPERTS:N_EXPERTS + N_GROUPS].set(w_coarse[l])
    wr_hi = wr.astype(BF)
    wr_lo = (wr - wr_hi.astype(F32)).astype(BF)
    br = jnp.zeros((1, LANES), F32).at[0, :N_EXPERTS].set(b_fine[l]).at[0, N_EXPERTS:N_EXPERTS + N_GROUPS].set(b_coarse[l])
    post_args = (row(conv_ln_g[l]), row(conv_ln_b[l]), w_conv_out[l].astype(BF), row(ret_gn_g[l]),
                 w_ret_out[l].astype(BF), row(b_gates[l]), w_o[l].astype(BF), row(norm_ffn_g[l]),
                 wr_hi, wr_lo, br)
    wg = w_gate_e[l].astype(BF)
    wu = w_up_e[l].astype(BF)
    wd = w_down_e[l].astype(BF)
    mix_g = row(norm_mix_g[l])
    cw = conv_w[l]
    cb = row(conv_b[l])

    small = dict(zip(_POST_NAMES, post_args))
    small_args = tuple(small[k] for k in _SMALL_NAMES)
    big_args = (w_in_bf,) + tuple(small[k] for k in _BIG_NAMES[1:])
    r = np.arange(PT)
    tri = jnp.asarray(r[None, :] < r[:, None], BF)
    consts = (cw, cb, tri, mix_g)
    decay = _decay_tables(PT)
    x_meta = jnp.concatenate([jnp.zeros((PT - N_META, D), F32), meta_tokens])[None]
    pos_meta = np.maximum(np.arange(PT) - (PT - N_META), 0)
    _, _, _, s_meta, buf_meta = _prompt_call(
        x_meta, jnp.zeros((N_HEADS, DK, DV), F32), jnp.zeros((HALO, D), F32),
        _rope_tables(pos_meta) + decay, consts, small_args, big_args, "meta_state")
    pos_p = N_META + np.arange(seq)
    rows_p, route_p, counts, s_new_p, buf_p = _prompt_call(
        x_prompt, s_meta[0], buf_meta[0], _rope_tables(pos_p) + decay, consts, small_args, big_args,
        "prompt_mix")

    xs = x_sample.reshape(ns, D)
    proj_s = _proj_call(xs, mix_g, w_in_bf, tm=ns)
    pos_s = np.full((1,), PAST_LEN)
    cos_s, sin_s = _rope_tables(pos_s)
    _, _, _, gam = _decay_tables(1)

    y_p, o_s, c_s, s_new_s, buf_s = _moe_sparse_call(
        rows_p.reshape(bp * seq, ROW_W), route_p,
        counts[0, :N_BINS].astype(jnp.int32), row(norm_ffn_g[l]), row(norm_final_g), wg, wu, wd,
        (proj_s, state_ret[l], state_conv[l], cos_s, sin_s, gam, cw, cb))

    x1_s, comb_s = _post_call(c_s, o_s, proj_s, xs, post_args)
    y_s = _moe_call(x1_s, comb_s, row(norm_ffn_g[l]), row(norm_final_g), wg, wu, wd, tm=ns).reshape(ns, 1, D)

    return (y_p.reshape(bp, seq, D), y_s, s_new_p[None], buf_p[:, HALO - (CONV_W - 1):][None],
            s_new_s[None], buf_s[None])
```

```python
import functools

import jax
import jax.numpy as jnp
import numpy as np
from jax import lax
from jax.experimental import pallas as pl
from jax.experimental.pallas import tpu as pltpu

D = 1024
N_META = 16
PAST_LEN = 16384
CONV_W = 31
N_HEADS = 4
DK = 256
DV = 512
D_QK = N_HEADS * DK
D_V = N_HEADS * DV
ROPE_BASE = 10000.0
N_GROUPS = 4
EPG = 4
N_EXPERTS = 16
D_EXPERT = 512
EPS = 1e-6
D_IN = 2 * D + 2 * D_QK + 2 * D_V + 2 * D

O_GLU_A, O_GLU_B = 0, D
O_Q, O_K = 2 * D, 2 * D + D_QK
O_V = 2 * D + 2 * D_QK
O_GRET = O_V + D_V
O_GA = O_GRET + D_V
O_GB = O_GA + D

N_PAIRS = EPG * (EPG - 1) // 2
N_BINS = N_GROUPS * N_PAIRS

LANES = 128
SUBLANES = 8
MXU_DIM = 256
ROW_W = D + LANES
AUX_W_LO, AUX_W_HI = 0, 1
ROUTE_BIN, ROUTE_RANK = 0, 1
MOE_TM = MXU_DIM
FILL_UNROLL = 8
PT = MXU_DIM
CONV_COLS = MXU_DIM
HALO = 32
VMEM_LIMIT = 56 * 1024 * 1024

BF = jnp.bfloat16
F32 = jnp.float32


def _sigmoid(x):
    return 1.0 / (1.0 + jnp.exp(-x))


def _bdot(a, b):
    return jnp.dot(a.astype(BF), b.astype(BF), preferred_element_type=F32)


def _rmsnorm(x, g):
    return x * lax.rsqrt(jnp.mean(x * x, axis=-1, keepdims=True) + EPS) * g


def _proj_kernel(x_ref, g_ref, w_ref, o_ref):
    h = _rmsnorm(x_ref[...], g_ref[...])
    o_ref[...] = jnp.dot(h.astype(BF), w_ref[...], preferred_element_type=F32)


def _proj_call(x2d, g, w_bf, tm, tn=2048):
    n = x2d.shape[0]
    return pl.pallas_call(
        _proj_kernel,
        grid=(D_IN // tn, n // tm),
        in_specs=[
            pl.BlockSpec((tm, D), lambda j, i: (i, 0)),
            pl.BlockSpec((1, D), lambda j, i: (0, 0)),
            pl.BlockSpec((D, tn), lambda j, i: (0, j)),
        ],
        out_specs=pl.BlockSpec((tm, tn), lambda j, i: (i, j)),
        out_shape=jax.ShapeDtypeStruct((n, D_IN), F32),
        compiler_params=pltpu.CompilerParams(
            dimension_semantics=("arbitrary", "arbitrary"), vmem_limit_bytes=VMEM_LIMIT),
        name="in_proj",
    )(x2d, g, w_bf)


def _route(t, wr_hi_ref, wr_lo_ref, br_ref):
    t_hi = t.astype(BF)
    t_lo = (t - t_hi.astype(F32)).astype(BF)
    wh = wr_hi_ref[...]
    logits = (jnp.dot(t_hi, wh, preferred_element_type=F32)
              + jnp.dot(t_hi, wr_lo_ref[...], preferred_element_type=F32)
              + jnp.dot(t_lo, wh, preferred_element_type=F32)) + br_ref[...]
    lt = logits.T
    row = lambda r: lt[r:r + 1, :]
    neg = jnp.float32(-jnp.inf)
    cmax, g_sel = _first_max([row(N_EXPERTS + k) for k in range(N_GROUPS)])
    csum = sum(jnp.exp(row(N_EXPERTS + k) - cmax) for k in range(N_GROUPS))
    p_g = 1.0 / csum
    fine = []
    for k in range(EPG):
        cand = [row(grp * EPG + k) for grp in range(N_GROUPS)]
        fine.append(jnp.where(g_sel == 0, cand[0], jnp.where(g_sel == 1, cand[1],
                                                             jnp.where(g_sel == 2, cand[2], cand[3]))))
    v1, j1 = _first_max(fine)
    v2, j2 = _first_max([jnp.where(j1 == k, neg, fine[k]) for k in range(EPG)])
    e2 = jnp.exp(v2 - v1)
    w1 = p_g / (1.0 + e2)
    w2 = e2 * w1
    first_low = j1 < j2
    a = jnp.where(first_low, j1, j2)
    b = jnp.where(first_low, j2, j1)
    pair = jnp.where(a == 0, 0, jnp.where(a == 1, 3, 5)) + (b - a - 1)
    bin_id = g_sel * N_PAIRS + pair
    w_lo = jnp.where(first_low, w1, w2)
    w_hi = jnp.where(first_low, w2, w1)
    sub = lax.broadcasted_iota(jnp.int32, lt.shape, 0)
    comb = jnp.where(sub == g_sel * EPG + j1, w1, jnp.where(sub == g_sel * EPG + j2, w2, 0.0)).T
    rec = jnp.where(sub == 0, w_lo, jnp.where(sub == 1, w_hi,
                                              jnp.where(sub == 2, bin_id.astype(F32), 0.0))).T
    return comb, (rec, bin_id)


def _first_max(vals):
    b01 = vals[1] > vals[0]
    b23 = vals[3] > vals[2]
    m01 = jnp.where(b01, vals[1], vals[0])
    m23 = jnp.where(b23, vals[3], vals[2])
    top = m23 > m01
    idx = jnp.where(top, jnp.where(b23, 3, 2), jnp.where(b01, 1, 0)).astype(jnp.int32)
    return jnp.where(top, m23, m01), idx


def _silu(x):
    return x * _sigmoid(x)


def _post_tokens(c, o, gret_act, g_a, g_b, x, p):
    mu = jnp.mean(c, axis=-1, keepdims=True)
    dc = c - mu
    var = jnp.mean(dc * dc, axis=-1, keepdims=True)
    cn = dc * lax.rsqrt(var + EPS) * p["ln_g"][...] + p["ln_b"][...]
    y_a = _bdot(_silu(cn), p["w_conv_out"][...])
    parts = []
    for h in range(N_HEADS):
        oh = o[:, h * DV:(h + 1) * DV]
        omu = jnp.mean(oh, axis=-1, keepdims=True)
        od = oh - omu
        ovar = jnp.mean(od * od, axis=-1, keepdims=True)
        parts.append(od * lax.rsqrt(ovar + EPS))
    on = jnp.concatenate(parts, axis=-1) * p["gn_g"][...]
    y_b = _bdot(on * gret_act, p["w_ret_out"][...])
    x1 = x + _bdot(g_a * y_a + g_b * y_b, p["w_o"][...])
    t = _rmsnorm(x1, p["ffn_g"][...])
    comb, pair_route = _route(t, p["wr_hi"], p["wr_lo"], p["br"])
    return x1, comb, pair_route


_POST_NAMES = ("ln_g", "ln_b", "w_conv_out", "gn_g", "w_ret_out", "b_gates", "w_o", "ffn_g",
               "wr_hi", "wr_lo", "br")


def _post_specs():
    z2 = lambda *_: (0, 0)
    shapes = {"ln_g": (1, D), "ln_b": (1, D), "w_conv_out": (D, D), "gn_g": (1, D_V),
              "w_ret_out": (D_V, D), "b_gates": (1, 2 * D), "w_o": (D, D), "ffn_g": (1, D),
              "wr_hi": (D, LANES), "wr_lo": (D, LANES), "br": (1, LANES)}
    return [pl.BlockSpec(shapes[k], z2) for k in _POST_NAMES]


def _rotary(x, cos, sin):
    x1, x2 = x[:, :LANES], x[:, LANES:]
    return jnp.concatenate([x1 * cos - x2 * sin, x1 * sin + x2 * cos], axis=-1)


_SMALL_NAMES = ("ln_g", "ln_b", "gn_g", "b_gates", "ffn_g", "wr_hi", "wr_lo", "br")
_BIG_NAMES = ("w_in", "w_conv_out", "w_ret_out", "w_o")


def _prompt_kernel(x_ref, buf0_ref, cos_ref, sin_ref, dmat_ref, dq_ref, dk_ref, gl_ref,
                   cw_ref, cb_ref, tri_ref, mixg_ref, *rest):
    ns, nb = len(_SMALL_NAMES), len(_BIG_NAMES)
    small = dict(zip(_SMALL_NAMES, rest[:ns]))
    s0_hbm = rest[ns]
    big_hbm = rest[ns + 1:ns + 1 + nb]
    x1e_ref, route_ref, cnt_out_ref, s_out_hbm, buf_out_ref = rest[ns + 1 + nb:ns + 6 + nb]
    scratch = rest[ns + 6 + nb:]
    big = dict(zip(_BIG_NAMES, scratch[:nb]))
    wsem, ssem, shift_ref, tail_ref, s_ref, c_ref, o_ref, h_ref, cnt_ref = scratch[nb:]
    p = dict(small, **{k: big[k] for k in _BIG_NAMES[1:]})
    w_in = big["w_in"]
    b = pl.program_id(0)
    i = pl.program_id(1)

    @pl.when((b == 0) & (i == 0))
    def _():
        copies = [pltpu.make_async_copy(src, big[k], wsem.at[n])
                  for n, (k, src) in enumerate(zip(_BIG_NAMES, big_hbm))]
        for cp in copies:
            cp.start()
        for cp in copies:
            cp.wait()
        cnt_ref[...] = jnp.zeros_like(cnt_ref)

    @pl.when(i == 0)
    def _():
        load_state = pltpu.make_async_copy(s0_hbm, s_ref, ssem)
        load_state.start()
        load_state.wait()
        tail_ref[...] = buf0_ref[...]

    h_ref[...] = _rmsnorm(x_ref[0], mixg_ref[...]).astype(BF)

    def proj(lo, width):
        return jnp.dot(h_ref[...], w_in[:, lo:lo + width], preferred_element_type=F32)

    base = HALO - (CONV_W - 1)

    def conv_block(cb):
        c0 = cb * CONV_COLS
        par = cb % 2
        u = proj(O_GLU_A + c0, CONV_COLS) * _sigmoid(proj(O_GLU_B + c0, CONV_COLS))
        for half in range(CONV_COLS // LANES):
            cols = slice(c0 + half * LANES, c0 + (half + 1) * LANES)
            uh = u[:, half * LANES:(half + 1) * LANES]
            for s in range(SUBLANES):
                shift_ref[par, s, half, 0:HALO - s, :] = tail_ref[s:HALO, cols]
                shift_ref[par, s, half, HALO - s:HALO - s + PT, :] = uh
            tail_ref[:, cols] = uh[PT - HALO:PT, :]
            acc = jnp.broadcast_to(cb_ref[:, cols], (PT, LANES))
            for j in range(CONV_W):
                s = (base + j) % SUBLANES
                start = base + j - s
                acc = acc + cw_ref[j:j + 1, cols] * shift_ref[par, s, half, start:start + PT, :]
            c_ref[:, cols] = acc

    cos = cos_ref[...]
    sin = sin_ref[...]

    def retention_head(h):
        q = _rotary(proj(O_Q + h * DK, DK), cos, sin)
        k = _rotary(proj(O_K + h * DK, DK), cos, sin) * (DK ** -0.5)
        v = proj(O_V + h * DV, DV).astype(BF)
        dq = dq_ref[h]
        dk = dk_ref[h]
        scores = lax.dot_general(q.astype(BF), k.astype(BF), (((1,), (1,)), ((), ())),
                                 preferred_element_type=F32) * dmat_ref[h]
        s_old = s_ref[h]
        q_dec = jnp.concatenate([q[:, :LANES] * dq, q[:, LANES:] * dq], axis=-1)
        k_dec = jnp.concatenate([k[:, :LANES] * dk, k[:, LANES:] * dk], axis=-1)
        o_ref[:, h * DV:(h + 1) * DV] = (jnp.dot(scores.astype(BF), v, preferred_element_type=F32)
                                         + _bdot(q_dec, s_old))
        s_ref[h] = gl_ref[h] * s_old + lax.dot_general(
            k_dec.astype(BF), v, (((0,), (0,)), ((), ())), preferred_element_type=F32)

    assert D // CONV_COLS == N_HEADS
    for n in range(N_HEADS):
        conv_block(n)
        retention_head(n)

    @pl.when(i == pl.num_programs(1) - 1)
    def _():
        store_state = pltpu.make_async_copy(s_ref, s_out_hbm.at[b], ssem)
        store_state.start()
        store_state.wait()
        buf_out_ref[0] = tail_ref[...]

    bg = small["b_gates"]
    x1, _, (rec, bin_row) = _post_tokens(c_ref[...], o_ref[...], _silu(proj(O_GRET, D_V)),
                                         _sigmoid(proj(O_GA, D) + bg[:, :D]),
                                         _sigmoid(proj(O_GB, D) + bg[:, D:]), x_ref[0], p)

    sub = lax.broadcasted_iota(jnp.int32, (LANES, PT), 0)
    onehot = sub == bin_row
    earlier = lax.dot_general(onehot.astype(BF), tri_ref[...], (((1,), (1,)), ((), ())),
                              preferred_element_type=F32)
    rank_row = jnp.sum(jnp.where(onehot, earlier + cnt_ref[...], 0.0), axis=0, keepdims=True)
    cnt_ref[...] += jnp.sum(onehot.astype(F32), axis=1, keepdims=True)

    lane = lax.broadcasted_iota(jnp.int32, (PT, LANES), 1)
    x1e_ref[0, :, 0:D] = x1
    x1e_ref[0, :, D:ROW_W] = jnp.where(lane <= AUX_W_HI, rec, 0.0)
    row8 = lax.broadcasted_iota(jnp.int32, (SUBLANES, PT), 0)
    route_ref[...] = jnp.where(row8 == ROUTE_BIN, bin_row,
                               jnp.where(row8 == ROUTE_RANK, rank_row.astype(jnp.int32), 0))
    cnt_out_ref[...] = cnt_ref[...]


def _prompt_call(x, s0, buf0, tabs, consts, small, big, name):
    b, t, _ = x.shape
    cos, sin, dmat, dq, dk, gl = tabs
    z2 = lambda bi, i: (0, 0)
    z3 = lambda bi, i: (0, 0, 0)
    tok = lambda bi, i: (bi, i, 0)
    whole = lambda a: pl.BlockSpec(a.shape, z2 if a.ndim == 2 else z3)
    any_spec = pl.BlockSpec(memory_space=pl.ANY)
    return pl.pallas_call(
        _prompt_kernel,
        grid=(b, t // PT),
        in_specs=[
            pl.BlockSpec((1, PT, D), tok),
            whole(buf0),
            pl.BlockSpec((PT, LANES), lambda bi, i: (i, 0)),
            pl.BlockSpec((PT, LANES), lambda bi, i: (i, 0)),
            whole(dmat), whole(dq), whole(dk),
            pl.BlockSpec(memory_space=pltpu.SMEM),
        ] + [whole(a) for a in consts] + [whole(a) for a in small] + [any_spec] * (1 + len(big)),
        out_specs=[
            pl.BlockSpec((1, PT, ROW_W), tok),
            pl.BlockSpec((SUBLANES, PT), lambda bi, i: (0, bi * (t // PT) + i)),
            pl.BlockSpec((LANES, 1), z2),
            any_spec,
            pl.BlockSpec((1, HALO, D), lambda bi, i: (bi, 0, 0)),
        ],
        out_shape=[
            jax.ShapeDtypeStruct((b, t, ROW_W), F32),
            jax.ShapeDtypeStruct((SUBLANES, b * t), jnp.int32),
            jax.ShapeDtypeStruct((LANES, 1), F32),
            jax.ShapeDtypeStruct((b, N_HEADS, DK, DV), F32),
            jax.ShapeDtypeStruct((b, HALO, D), F32),
        ],
        scratch_shapes=[pltpu.VMEM(w.shape, BF) for w in big] + [
            pltpu.SemaphoreType.DMA((len(big),)),
            pltpu.SemaphoreType.DMA,
            pltpu.VMEM((2, SUBLANES, CONV_COLS // LANES, HALO + PT, LANES), F32),
            pltpu.VMEM((HALO, D), F32),
            pltpu.VMEM((N_HEADS, DK, DV), F32),
            pltpu.VMEM((PT, D), F32),
            pltpu.VMEM((PT, D_V), F32),
            pltpu.VMEM((PT, D), BF),
            pltpu.VMEM((LANES, 1), F32),
        ],
        compiler_params=pltpu.CompilerParams(
            dimension_semantics=("arbitrary", "arbitrary"), vmem_limit_bytes=VMEM_LIMIT),
        name=name,
    )(x, buf0, cos, sin, dmat, dq, dk, gl, *consts, *small, s0, *big)


SEQ_BLK = 2
MXU_ROWS = 16


def _sample_seq_step(step, proj_ref, s_ref, sc_ref, cos_ref, sin_ref, gam_ref, cw_ref, cb_ref,
                     o_ref, c_ref, s_out_ref, sc_out_ref):
    cos = cos_ref[...]
    sin = sin_ref[...]
    row = lax.broadcasted_iota(jnp.int32, (MXU_ROWS, DK), 0)
    for s in range(SEQ_BLK):
        seq = pl.ds(step * SEQ_BLK + s, 1)
        us = proj_ref[seq, O_GLU_A:O_GLU_A + D] * _sigmoid(proj_ref[seq, O_GLU_B:O_GLU_B + D])
        c_ref[seq, :] = (
            jnp.sum(cw_ref[0:CONV_W - 1, :] * sc_ref[s], axis=0, keepdims=True)
            + cw_ref[CONV_W - 1:CONV_W, :] * us + cb_ref[...])
        sc_out_ref[s, 0:CONV_W - 2, :] = sc_ref[s, 1:CONV_W - 1, :]
        sc_out_ref[s, CONV_W - 2:CONV_W - 1, :] = us
        for h in range(N_HEADS):
            q = _rotary(proj_ref[seq, O_Q + h * DK:O_Q + (h + 1) * DK], cos, sin)
            k = _rotary(proj_ref[seq, O_K + h * DK:O_K + (h + 1) * DK], cos, sin) * (DK ** -0.5)
            v = proj_ref[seq, O_V + h * DV:O_V + (h + 1) * DV]
            gam = gam_ref[h]
            qk = jnp.sum(q * k, axis=-1, keepdims=True)
            s_old = s_ref[s, h]
            o8 = _bdot(jnp.broadcast_to(q * gam, (MXU_ROWS, DK)), s_old)
            o_ref[seq, h * DV:(h + 1) * DV] = qk * v + o8[0:1, :]
            k8 = jnp.where(row == 0, jnp.broadcast_to(k, (MXU_ROWS, DK)), 0.0).astype(BF)
            v8 = jnp.broadcast_to(v, (MXU_ROWS, DV)).astype(BF)
            kv = lax.dot_general(k8, v8, (((0,), (0,)), ((), ())), preferred_element_type=F32)
            s_out_ref[s, h] = gam * s_old + kv


def _post_kernel(c_ref, o_ref, gret_ref, gates_ref, x_ref, *rest):
    p = dict(zip(_POST_NAMES, rest[:len(_POST_NAMES)]))
    x1_ref, comb_ref = rest[len(_POST_NAMES):]
    bg = p["b_gates"][...]
    x1, comb, _ = _post_tokens(c_ref[...], o_ref[...], _silu(gret_ref[...]),
                               _sigmoid(gates_ref[:, :D] + bg[:, :D]), _sigmoid(gates_ref[:, D:] + bg[:, D:]),
                               x_ref[...], p)
    x1_ref[...] = x1
    comb_ref[...] = comb


def _post_call(c, o, proj_s, x_s, post_args):
    n = x_s.shape[0]
    z2 = lambda i: (0, 0)
    return pl.pallas_call(
        _post_kernel,
        grid=(1,),
        in_specs=[
            pl.BlockSpec((n, D), z2),
            pl.BlockSpec((n, D_V), z2),
            pl.BlockSpec((n, D_V), lambda i: (0, O_GRET // D_V)),
            pl.BlockSpec((n, 2 * D), lambda i: (0, O_GA // (2 * D))),
            pl.BlockSpec((n, D), z2),
        ] + _post_specs(),
        out_specs=[pl.BlockSpec((n, D), z2), pl.BlockSpec((n, LANES), z2)],
        out_shape=[jax.ShapeDtypeStruct((n, D), F32), jax.ShapeDtypeStruct((n, LANES), F32)],
        compiler_params=pltpu.CompilerParams(
            dimension_semantics=("arbitrary",), vmem_limit_bytes=VMEM_LIMIT),
        name="sample_post",
    )(c, o, proj_s, proj_s, x_s, *post_args)


def _moe_kernel(x1_ref, comb_ref, fg_ref, ng_ref, wg_ref, wu_ref, wd_ref, y_ref, t_ref, acc_ref):
    g = pl.program_id(1)

    @pl.when(g == 0)
    def _():
        t_ref[...] = _rmsnorm(x1_ref[...], fg_ref[...]).astype(BF)
        acc_ref[...] = jnp.zeros_like(acc_ref)

    t = t_ref[...]
    comb = comb_ref[...]
    lane = lax.broadcasted_iota(jnp.int32, comb.shape, 1)
    acc = acc_ref[...]
    for k in range(EPG):
        hg = jnp.dot(t, wg_ref[k], preferred_element_type=F32)
        hu = jnp.dot(t, wu_ref[k], preferred_element_type=F32)
        ce = jnp.sum(jnp.where(lane == g * EPG + k, comb, 0.0), axis=-1, keepdims=True)
        act = (hg * _sigmoid(hg)) * hu * ce
        acc = acc + jnp.dot(act.astype(BF), wd_ref[k], preferred_element_type=F32)
    acc_ref[...] = acc

    @pl.when(g == N_GROUPS - 1)
    def _():
        y_ref[...] = _rmsnorm(x1_ref[...] + acc, ng_ref[...])


def _moe_call(x1, comb, ffn_g, final_g, wg, wu, wd, tm):
    n = x1.shape[0]
    z2 = lambda i, e: (0, 0)
    return pl.pallas_call(
        _moe_kernel,
        grid=(n // tm, N_GROUPS),
        in_specs=[
            pl.BlockSpec((tm, D), lambda i, e: (i, 0)),
            pl.BlockSpec((tm, LANES), lambda i, e: (i, 0)),
            pl.BlockSpec((1, D), z2),
            pl.BlockSpec((1, D), z2),
            pl.BlockSpec((EPG, D, D_EXPERT), lambda i, e: (e, 0, 0)),
            pl.BlockSpec((EPG, D, D_EXPERT), lambda i, e: (e, 0, 0)),
            pl.BlockSpec((EPG, D_EXPERT, D), lambda i, e: (e, 0, 0)),
        ],
        out_specs=pl.BlockSpec((tm, D), lambda i, e: (i, 0)),
        out_shape=jax.ShapeDtypeStruct((n, D), F32),
        scratch_shapes=[pltpu.VMEM((tm, D), BF), pltpu.VMEM((tm, D), F32)],
        compiler_params=pltpu.CompilerParams(
            dimension_semantics=("arbitrary", "arbitrary"), vmem_limit_bytes=VMEM_LIMIT),
        name="moe",
    )(x1, comb, ffn_g, final_g, wg, wu, wd)


def _moe_sparse_kernel(n_seq_steps, slot_ref, e_lo_ref, e_hi_ref, nvalid_ref, nused_ref,
                       rows_hbm, fg_ref, ng_ref, wg_lo, wu_lo, wd_lo, wg_hi, wu_hi, wd_hi,
                       dproj_ref, ds_ref, dsc_ref, dcos_ref, dsin_ref, dgam_ref, dcw_ref, dcb_ref,
                       y_hbm, do_ref, dc_ref, ds_out_ref, dsc_out_ref,
                       inv_ref, xbuf, ybuf, gsem, ssem):
    t = pl.program_id(0)
    n_used = nused_ref[0]
    n_tok = slot_ref.shape[0]
    cur = lax.rem(t, 2)

    @pl.when(t < n_seq_steps)
    def _():
        _sample_seq_step(t, dproj_ref, ds_ref, dsc_ref, dcos_ref, dsin_ref, dgam_ref, dcw_ref, dcb_ref,
                         do_ref, dc_ref, ds_out_ref, dsc_out_ref)

    def hbm_row(ref, tok):
        return ref.at[jnp.right_shift(tok, 3), pl.ds(tok & (SUBLANES - 1), 1), :]

    def gather_copy(grp, sub, tok, buf):
        return pltpu.make_async_copy(hbm_row(rows_hbm, tok), xbuf.at[buf, grp, pl.ds(sub, 1), :],
                                     gsem.at[buf])

    def scatter_copy(grp, sub, tok, buf):
        return pltpu.make_async_copy(ybuf.at[buf, grp, pl.ds(sub, 1), :], hbm_row(y_hbm, tok),
                                     ssem.at[buf])

    def start_rows(copy, tile, buf):
        base = tile * MOE_TM
        n_rows = nvalid_ref[tile]
        n_grp = n_rows // SUBLANES

        def group(g, c):
            for u in range(SUBLANES):
                copy(g, u, inv_ref[base + g * SUBLANES + u], buf).start()
            return c
        lax.fori_loop(0, n_grp, group, 0)

        def tail(r, c):
            copy(n_grp, r - n_grp * SUBLANES, inv_ref[base + r], buf).start()
            return c
        lax.fori_loop(n_grp * SUBLANES, n_rows, tail, 0)

    def wait_rows(copy, tile, buf):
        n_rows = nvalid_ref[tile]
        vbuf, hbm, sem = (xbuf, rows_hbm, gsem) if copy is gather_copy else (ybuf, y_hbm, ssem)
        k = MOE_TM
        while k >= 1:
            @pl.when((n_rows & k) != 0)
            def _(k=k):
                if k >= SUBLANES:
                    vmem = vbuf.at[buf, pl.ds(0, k // SUBLANES)]
                    ext = hbm.at[pl.ds(0, k // SUBLANES)]
                else:
                    vmem = vbuf.at[buf, 0, pl.ds(0, k), :]
                    ext = hbm.at[0, pl.ds(0, k), :]
                pair = (ext, vmem) if copy is gather_copy else (vmem, ext)
                pltpu.make_async_copy(pair[0], pair[1], sem.at[buf]).wait()
            k //= 2

    @pl.when(t == 0)
    def _():
        def fill(tok, c):
            inv_ref[slot_ref[tok]] = tok
            return c
        lax.fori_loop(0, n_tok, fill, 0, unroll=FILL_UNROLL)
        xbuf[...] = jnp.zeros_like(xbuf)
        start_rows(gather_copy, 0, 0)

    @pl.when(t < n_used)
    def _():
        wait_rows(gather_copy, t, cur)

        @pl.when(t + 1 < n_used)
        def _():
            start_rows(gather_copy, t + 1, 1 - cur)

        xt = xbuf[cur].reshape(MOE_TM, ROW_W)
        x1 = xt[:, 0:D]
        aux = xt[:, D:ROW_W]
        tb = _rmsnorm(x1, fg_ref[...]).astype(BF)
        acc = None
        for wg, wu, wd, lane_w in ((wg_lo, wu_lo, wd_lo, AUX_W_LO), (wg_hi, wu_hi, wd_hi, AUX_W_HI)):
            hg = jnp.dot(tb, wg[0], preferred_element_type=F32)
            hu = jnp.dot(tb, wu[0], preferred_element_type=F32)
            act = (hg * _sigmoid(hg)) * hu * aux[:, lane_w:lane_w + 1]
            part = jnp.dot(act.astype(BF), wd[0], preferred_element_type=F32)
            acc = part if acc is None else acc + part
        y = _rmsnorm(x1 + acc, ng_ref[...])

        @pl.when(t >= 2)
        def _():
            wait_rows(scatter_copy, jnp.maximum(t - 2, 0), cur)
        ybuf[cur] = y.reshape(MOE_TM // SUBLANES, SUBLANES, D)
        start_rows(scatter_copy, t, cur)

        @pl.when(t == n_used - 1)
        def _():
            wait_rows(scatter_copy, t, cur)

            @pl.when(t >= 1)
            def _():
                wait_rows(scatter_copy, jnp.maximum(t - 1, 0), 1 - cur)


def _bin_tables():
    lo, hi = [], []
    for g in range(N_GROUPS):
        for a in range(EPG):
            for b in range(a + 1, EPG):
                lo.append(g * EPG + a)
                hi.append(g * EPG + b)
    return jnp.array(lo, jnp.int32), jnp.array(hi, jnp.int32)


def _moe_sparse_call(rows, route, counts, ffn_g, final_g, wg, wu, wd, decode):
    proj_s, state_ret, state_conv, cos_s, sin_s, gam, cw, cb = decode
    n_seq = proj_s.shape[0]
    n_seq_steps = n_seq // SEQ_BLK
    n = route.shape[1]
    max_tiles = n // MOE_TM + N_BINS
    assert max_tiles >= n_seq_steps
    padded = ((counts + MOE_TM - 1) // MOE_TM) * MOE_TM
    ends = jnp.cumsum(padded)
    offs = ends - padded
    n_used = ends[-1] // MOE_TM
    slot = offs[route[ROUTE_BIN]] + route[ROUTE_RANK]
    tile_start = jnp.minimum(jnp.arange(max_tiles, dtype=jnp.int32), n_used - 1) * MOE_TM
    tile_bin = jnp.sum((tile_start[:, None] >= ends[None, :]).astype(jnp.int32), axis=1)
    bin_lo, bin_hi = _bin_tables()
    e_lo = bin_lo[tile_bin]
    e_hi = bin_hi[tile_bin]
    tile_id = jnp.arange(max_tiles, dtype=jnp.int32)
    n_valid = jnp.where(tile_id < n_used,
                        jnp.clip(counts[tile_bin] - (tile_start - offs[tile_bin]), 0, MOE_TM), 0)

    z2 = lambda t, *_: (0, 0)
    w_lo_map = lambda t, slot_r, lo_r, hi_r, nv_r, nu_r: (lo_r[t], 0, 0)
    w_hi_map = lambda t, slot_r, lo_r, hi_r, nv_r, nu_r: (hi_r[t], 0, 0)
    seq3 = lambda t, *_: (jnp.minimum(t, n_seq_steps - 1), 0, 0)
    seq4 = lambda t, *_: (jnp.minimum(t, n_seq_steps - 1), 0, 0, 0)
    y, o, c, s_new, sc_new = pl.pallas_call(
        functools.partial(_moe_sparse_kernel, n_seq_steps),
        grid_spec=pltpu.PrefetchScalarGridSpec(
            num_scalar_prefetch=5,
            grid=(max_tiles,),
            in_specs=[
                pl.BlockSpec(memory_space=pl.ANY),
                pl.BlockSpec((1, D), z2),
                pl.BlockSpec((1, D), z2),
                pl.BlockSpec((1, D, D_EXPERT), w_lo_map),
                pl.BlockSpec((1, D, D_EXPERT), w_lo_map),
                pl.BlockSpec((1, D_EXPERT, D), w_lo_map),
                pl.BlockSpec((1, D, D_EXPERT), w_hi_map),
                pl.BlockSpec((1, D, D_EXPERT), w_hi_map),
                pl.BlockSpec((1, D_EXPERT, D), w_hi_map),
                pl.BlockSpec((n_seq, D_IN), z2),
                pl.BlockSpec((SEQ_BLK, N_HEADS, DK, DV), seq4),
                pl.BlockSpec((SEQ_BLK, CONV_W - 1, D), seq3),
                pl.BlockSpec((1, LANES), z2),
                pl.BlockSpec((1, LANES), z2),
                pl.BlockSpec(memory_space=pltpu.SMEM),
                pl.BlockSpec((CONV_W, D), z2),
                pl.BlockSpec((1, D), z2),
            ],
            out_specs=[
                pl.BlockSpec(memory_space=pl.ANY),
                pl.BlockSpec((n_seq, D_V), z2),
                pl.BlockSpec((n_seq, D), z2),
                pl.BlockSpec((SEQ_BLK, N_HEADS, DK, DV), seq4),
                pl.BlockSpec((SEQ_BLK, CONV_W - 1, D), seq3),
            ],
            scratch_shapes=[
                pltpu.SMEM((max_tiles * MOE_TM,), jnp.int32),
                pltpu.VMEM((2, MOE_TM // SUBLANES, SUBLANES, ROW_W), F32),
                pltpu.VMEM((2, MOE_TM // SUBLANES, SUBLANES, D), F32),
                pltpu.SemaphoreType.DMA((2,)),
                pltpu.SemaphoreType.DMA((2,)),
            ],
        ),
        out_shape=[
            jax.ShapeDtypeStruct((n // SUBLANES, SUBLANES, D), F32),
            jax.ShapeDtypeStruct((n_seq, D_V), F32),
            jax.ShapeDtypeStruct((n_seq, D), F32),
            jax.ShapeDtypeStruct(state_ret.shape, F32),
            jax.ShapeDtypeStruct(state_conv.shape, F32),
        ],
        compiler_params=pltpu.CompilerParams(
            dimension_semantics=("arbitrary",), vmem_limit_bytes=VMEM_LIMIT),
        name="moe_sparse",
    )(slot, e_lo, e_hi, n_valid, n_used.reshape(1).astype(jnp.int32),
      rows.reshape(n // SUBLANES, SUBLANES, ROW_W), ffn_g, final_g, wg, wu, wd, wg, wu, wd,
      proj_s, state_ret, state_conv, cos_s, sin_s, gam, cw, cb)
    return y.reshape(n, D), o, c, s_new, sc_new


def _rope_tables(pos):
    half = DK // 2
    freqs = np.power(np.float64(ROPE_BASE), -np.arange(half, dtype=np.float64) / half)
    ang = pos.astype(np.float64)[:, None] * freqs[None, :]
    return jnp.asarray(np.cos(ang), F32), jnp.asarray(np.sin(ang), F32)


def _decay_tables(L):
    lg = np.log(1.0 - np.exp2(-5.0 - np.arange(N_HEADS, dtype=np.float64)))
    idx = np.arange(L, dtype=np.float64)
    diff = idx[:, None] - idx[None, :]
    dmat = np.where((diff >= 0)[None], np.exp(np.maximum(diff, 0.0)[None] * lg[:, None, None]), 0.0)
    dq = np.exp((idx[:, None] + 1.0) * lg[None, :]).T
    dk = np.exp((L - 1.0 - idx)[:, None] * lg[None, :]).T
    gl = np.exp(L * lg)
    bl = lambda a: jnp.asarray(np.broadcast_to(a[:, :, None], (N_HEADS, L, LANES)), F32)
    return jnp.asarray(dmat, F32), bl(dq), bl(dk), jnp.asarray(gl, F32)


def kernel(x_prompt, x_sample, state_ret, state_conv, meta_tokens, norm_mix_g, w_in, b_gates, conv_w, conv_b, conv_ln_g, conv_ln_b, w_conv_out, ret_gn_g, w_ret_out, w_o, norm_ffn_g, w_coarse, b_coarse, w_fine, b_fine, w_gate_e, w_up_e, w_down_e, norm_final_g):
    bp, seq, _ = x_prompt.shape
    ns = x_sample.shape[0]
    assert w_in.shape[0] == 1, "one layer"
    assert x_prompt.shape[2] == D and seq % PT == 0 and (bp * seq) % MOE_TM == 0
    assert x_sample.shape[1:] == (1, D) and ns % SEQ_BLK == 0 and ns % SUBLANES == 0
    assert meta_tokens.shape == (N_META, D) and N_META <= PT
    l = 0
    row = lambda a: a.reshape(1, -1)

    w_in_bf = w_in[l].astype(BF)
    wr = jnp.zeros((D, LANES), F32).at[:, :N_EXPERTS].set(w_fine[l]).at[:, N_EXPERTS:N_EXPERTS + N_GROUPS].set(w_coarse[l])
    wr_hi = wr.astype(BF)
    wr_lo = (wr - wr_hi.astype(F32)).astype(BF)
    br = jnp.zeros((1, LANES), F32).at[0, :N_EXPERTS].set(b_fine[l]).at[0, N_EXPERTS:N_EXPERTS + N_GROUPS].set(b_coarse[l])
    post_args = (row(conv_ln_g[l]), row(conv_ln_b[l]), w_conv_out[l].astype(BF), row(ret_gn_g[l]),
                 w_ret_out[l].astype(BF), row(b_gates[l]), w_o[l].astype(BF), row(norm_ffn_g[l]),
                 wr_hi, wr_lo, br)
    wg = w_gate_e[l].astype(BF)
    wu = w_up_e[l].astype(BF)
    wd = w_down_e[l].astype(BF)
    mix_g = row(norm_mix_g[l])
    cw = conv_w[l]
    cb = row(conv_b[l])

    small = dict(zip(_POST_NAMES, post_args))
    small_args = tuple(small[k] for k in _SMALL_NAMES)
    big_args = (w_in_bf,) + tuple(small[k] for k in _BIG_NAMES[1:])
    r = np.arange(PT)
    tri = jnp.asarray(r[None, :] < r[:, None], BF)
    consts = (cw, cb, tri, mix_g)
    decay = _decay_tables(PT)
    x_meta = jnp.concatenate([jnp.zeros((PT - N_META, D), F32), meta_tokens])[None]
    pos_meta = np.maximum(np.arange(PT) - (PT - N_META), 0)
    _, _, _, s_meta, buf_meta = _prompt_call(
        x_meta, jnp.zeros((N_HEADS, DK, DV), F32), jnp.zeros((HALO, D), F32),
        _rope_tables(pos_meta) + decay, consts, small_args, big_args, "meta_state")
    pos_p = N_META + np.arange(seq)
    rows_p, route_p, counts, s_new_p, buf_p = _prompt_call(
        x_prompt, s_meta[0], buf_meta[0], _rope_tables(pos_p) + decay, consts, small_args, big_args,
        "prompt_mix")

    xs = x_sample.reshape(ns, D)
    proj_s = _proj_call(xs, mix_g, w_in_bf, tm=ns)
    pos_s = np.full((1,), PAST_LEN)
    cos_s, sin_s = _rope_tables(pos_s)
    _, _, _, gam = _decay_tables(1)

    y_p, o_s, c_s, s_new_s, buf_s = _moe_sparse_call(
        rows_p.reshape(bp * seq, ROW_W), route_p,
        counts[:N_BINS, 0].astype(jnp.int32), row(norm_ffn_g[l]), row(norm_final_g), wg, wu, wd,
        (proj_s, state_ret[l], state_conv[l], cos_s, sin_s, gam, cw, cb))

    x1_s, comb_s = _post_call(c_s, o_s, proj_s, xs, post_args)
    y_s = _moe_call(x1_s, comb_s, row(norm_ffn_g[l]), row(norm_final_g), wg, wu, wd, tm=ns).reshape(ns, 1, D)

    return (y_p.reshape(bp, seq, D), y_s, s_new_p[None], buf_p[:, HALO - (CONV_W - 1):][None],
            s_new_s[None], buf_s[None])
```

```python
import functools

import jax
import jax.numpy as jnp
import numpy as np
from jax import lax
from jax.experimental import pallas as pl
from jax.experimental.pallas import tpu as pltpu

D = 1024
N_META = 16
PAST_LEN = 16384
CONV_W = 31
N_HEADS = 4
DK = 256
DV = 512
D_QK = N_HEADS * DK
D_V = N_HEADS * DV
ROPE_BASE = 10000.0
N_GROUPS = 4
EPG = 4
N_EXPERTS = 16
D_EXPERT = 512
EPS = 1e-6
D_IN = 2 * D + 2 * D_QK + 2 * D_V + 2 * D

O_GLU_A, O_GLU_B = 0, D
O_Q, O_K = 2 * D, 2 * D + D_QK
O_V = 2 * D + 2 * D_QK
O_GRET = O_V + D_V
O_GA = O_GRET + D_V
O_GB = O_GA + D

N_PAIRS = EPG * (EPG - 1) // 2
N_BINS = N_GROUPS * N_PAIRS

LANES = 128
SUBLANES = 8
MXU_DIM = 256
ROW_W = D + LANES
AUX_W_LO, AUX_W_HI = 0, 1
ROUTE_BIN, ROUTE_RANK = 0, 1
MOE_TM = MXU_DIM
FILL_UNROLL = 8
PT = MXU_DIM
CONV_COLS = MXU_DIM
CONV_ROWS = 32
HALO = 32
VMEM_LIMIT = 56 * 1024 * 1024

BF = jnp.bfloat16
F32 = jnp.float32


def _sigmoid(x):
    return 1.0 / (1.0 + jnp.exp(-x))


def _bdot(a, b):
    return jnp.dot(a.astype(BF), b.astype(BF), preferred_element_type=F32)


def _rmsnorm(x, g):
    return x * lax.rsqrt(jnp.mean(x * x, axis=-1, keepdims=True) + EPS) * g


def _proj_kernel(x_ref, g_ref, w_ref, o_ref):
    h = _rmsnorm(x_ref[...], g_ref[...])
    o_ref[...] = jnp.dot(h.astype(BF), w_ref[...], preferred_element_type=F32)


def _proj_call(x2d, g, w_bf, tm, tn=2048):
    n = x2d.shape[0]
    return pl.pallas_call(
        _proj_kernel,
        grid=(D_IN // tn, n // tm),
        in_specs=[
            pl.BlockSpec((tm, D), lambda j, i: (i, 0)),
            pl.BlockSpec((1, D), lambda j, i: (0, 0)),
            pl.BlockSpec((D, tn), lambda j, i: (0, j)),
        ],
        out_specs=pl.BlockSpec((tm, tn), lambda j, i: (i, j)),
        out_shape=jax.ShapeDtypeStruct((n, D_IN), F32),
        compiler_params=pltpu.CompilerParams(
            dimension_semantics=("arbitrary", "arbitrary"), vmem_limit_bytes=VMEM_LIMIT),
        name="in_proj",
    )(x2d, g, w_bf)


def _route(t, wr_hi_ref, wr_lo_ref, br_ref):
    t_hi = t.astype(BF)
    t_lo = (t - t_hi.astype(F32)).astype(BF)
    wh = wr_hi_ref[...]
    logits = (jnp.dot(t_hi, wh, preferred_element_type=F32)
              + jnp.dot(t_hi, wr_lo_ref[...], preferred_element_type=F32)
              + jnp.dot(t_lo, wh, preferred_element_type=F32)) + br_ref[...]
    lt = logits.T
    row = lambda r: lt[r:r + 1, :]
    neg = jnp.float32(-jnp.inf)
    cmax, g_sel = _first_max([row(N_EXPERTS + k) for k in range(N_GROUPS)])
    csum = sum(jnp.exp(row(N_EXPERTS + k) - cmax) for k in range(N_GROUPS))
    p_g = 1.0 / csum
    fine = []
    for k in range(EPG):
        cand = [row(grp * EPG + k) for grp in range(N_GROUPS)]
        fine.append(jnp.where(g_sel == 0, cand[0], jnp.where(g_sel == 1, cand[1],
                                                             jnp.where(g_sel == 2, cand[2], cand[3]))))
    v1, j1 = _first_max(fine)
    v2, j2 = _first_max([jnp.where(j1 == k, neg, fine[k]) for k in range(EPG)])
    e2 = jnp.exp(v2 - v1)
    w1 = p_g / (1.0 + e2)
    w2 = e2 * w1
    first_low = j1 < j2
    a = jnp.where(first_low, j1, j2)
    b = jnp.where(first_low, j2, j1)
    pair = jnp.where(a == 0, 0, jnp.where(a == 1, 3, 5)) + (b - a - 1)
    bin_id = g_sel * N_PAIRS + pair
    w_lo = jnp.where(first_low, w1, w2)
    w_hi = jnp.where(first_low, w2, w1)
    sub = lax.broadcasted_iota(jnp.int32, lt.shape, 0)
    comb = jnp.where(sub == g_sel * EPG + j1, w1, jnp.where(sub == g_sel * EPG + j2, w2, 0.0)).T
    rec = jnp.where(sub == 0, w_lo, jnp.where(sub == 1, w_hi,
                                              jnp.where(sub == 2, bin_id.astype(F32), 0.0))).T
    return comb, (rec, bin_id)


def _first_max(vals):
    b01 = vals[1] > vals[0]
    b23 = vals[3] > vals[2]
    m01 = jnp.where(b01, vals[1], vals[0])
    m23 = jnp.where(b23, vals[3], vals[2])
    top = m23 > m01
    idx = jnp.where(top, jnp.where(b23, 3, 2), jnp.where(b01, 1, 0)).astype(jnp.int32)
    return jnp.where(top, m23, m01), idx


def _silu(x):
    return x * _sigmoid(x)


def _post_tokens(c, o, gret_act, g_a, g_b, x, p):
    mu = jnp.mean(c, axis=-1, keepdims=True)
    dc = c - mu
    var = jnp.mean(dc * dc, axis=-1, keepdims=True)
    cn = dc * lax.rsqrt(var + EPS) * p["ln_g"][...] + p["ln_b"][...]
    y_a = _bdot(_silu(cn), p["w_conv_out"][...])
    parts = []
    for h in range(N_HEADS):
        oh = o[:, h * DV:(h + 1) * DV]
        omu = jnp.mean(oh, axis=-1, keepdims=True)
        od = oh - omu
        ovar = jnp.mean(od * od, axis=-1, keepdims=True)
        parts.append(od * lax.rsqrt(ovar + EPS))
    on = jnp.concatenate(parts, axis=-1) * p["gn_g"][...]
    y_b = _bdot(on * gret_act, p["w_ret_out"][...])
    x1 = x + _bdot(g_a * y_a + g_b * y_b, p["w_o"][...])
    t = _rmsnorm(x1, p["ffn_g"][...])
    comb, pair_route = _route(t, p["wr_hi"], p["wr_lo"], p["br"])
    return x1, comb, pair_route


_POST_NAMES = ("ln_g", "ln_b", "w_conv_out", "gn_g", "w_ret_out", "b_gates", "w_o", "ffn_g",
               "wr_hi", "wr_lo", "br")


def _post_specs():
    z2 = lambda *_: (0, 0)
    shapes = {"ln_g": (1, D), "ln_b": (1, D), "w_conv_out": (D, D), "gn_g": (1, D_V),
              "w_ret_out": (D_V, D), "b_gates": (1, 2 * D), "w_o": (D, D), "ffn_g": (1, D),
              "wr_hi": (D, LANES), "wr_lo": (D, LANES), "br": (1, LANES)}
    return [pl.BlockSpec(shapes[k], z2) for k in _POST_NAMES]


def _rotary(x, cos, sin):
    x1, x2 = x[:, :LANES], x[:, LANES:]
    return jnp.concatenate([x1 * cos - x2 * sin, x1 * sin + x2 * cos], axis=-1)


_SMALL_NAMES = ("ln_g", "ln_b", "gn_g", "b_gates", "ffn_g", "wr_hi", "wr_lo", "br")
_BIG_NAMES = ("w_in", "w_conv_out", "w_ret_out", "w_o")


def _prompt_kernel(x_ref, buf0_ref, cos_ref, sin_ref, dmat_ref, dq_ref, dk_ref, gl_ref,
                   cw_ref, cb_ref, tri_ref, mixg_ref, *rest):
    ns, nb = len(_SMALL_NAMES), len(_BIG_NAMES)
    small = dict(zip(_SMALL_NAMES, rest[:ns]))
    s0_hbm = rest[ns]
    big_hbm = rest[ns + 1:ns + 1 + nb]
    x1e_ref, route_ref, cnt_out_ref, s_out_hbm, buf_out_ref = rest[ns + 1 + nb:ns + 6 + nb]
    scratch = rest[ns + 6 + nb:]
    big = dict(zip(_BIG_NAMES, scratch[:nb]))
    wsem, ssem, shift_ref, tail_ref, s_ref, c_ref, o_ref, h_ref, cnt_ref = scratch[nb:]
    p = dict(small, **{k: big[k] for k in _BIG_NAMES[1:]})
    w_in = big["w_in"]
    b = pl.program_id(0)
    i = pl.program_id(1)

    @pl.when((b == 0) & (i == 0))
    def _():
        copies = [pltpu.make_async_copy(src, big[k], wsem.at[n])
                  for n, (k, src) in enumerate(zip(_BIG_NAMES, big_hbm))]
        for cp in copies:
            cp.start()
        for cp in copies:
            cp.wait()
        cnt_ref[...] = jnp.zeros_like(cnt_ref)

    @pl.when(i == 0)
    def _():
        load_state = pltpu.make_async_copy(s0_hbm, s_ref, ssem)
        load_state.start()
        load_state.wait()
        tail_ref[...] = buf0_ref[...]

    h_ref[...] = _rmsnorm(x_ref[0], mixg_ref[...]).astype(BF)

    def proj(lo, width):
        return jnp.dot(h_ref[...], w_in[:, lo:lo + width], preferred_element_type=F32)

    base = HALO - (CONV_W - 1)

    def conv_block(cb):
        c0 = cb * CONV_COLS
        par = cb % 2
        u = proj(O_GLU_A + c0, CONV_COLS) * _sigmoid(proj(O_GLU_B + c0, CONV_COLS))
        for half in range(CONV_COLS // LANES):
            cols = slice(c0 + half * LANES, c0 + (half + 1) * LANES)
            uh = u[:, half * LANES:(half + 1) * LANES]
            for s in range(SUBLANES):
                shift_ref[par, s, half, 0:HALO - s, :] = tail_ref[s:HALO, cols]
                shift_ref[par, s, half, HALO - s:HALO - s + PT, :] = uh
            tail_ref[:, cols] = uh[PT - HALO:PT, :]
            for r0 in range(0, PT, CONV_ROWS):
                acc = jnp.broadcast_to(cb_ref[:, cols], (CONV_ROWS, LANES))
                for j in range(CONV_W):
                    s = (base + j) % SUBLANES
                    start = base + j - s + r0
                    acc = acc + cw_ref[j:j + 1, cols] * shift_ref[par, s, half, start:start + CONV_ROWS, :]
                c_ref[r0:r0 + CONV_ROWS, cols] = acc

    cos = cos_ref[...]
    sin = sin_ref[...]

    def retention_head(h):
        q = _rotary(proj(O_Q + h * DK, DK), cos, sin)
        k = _rotary(proj(O_K + h * DK, DK), cos, sin) * (DK ** -0.5)
        v = proj(O_V + h * DV, DV).astype(BF)
        dq = dq_ref[h]
        dk = dk_ref[h]
        scores = lax.dot_general(q.astype(BF), k.astype(BF), (((1,), (1,)), ((), ())),
                                 preferred_element_type=F32) * dmat_ref[h]
        s_old = s_ref[h]
        q_dec = jnp.concatenate([q[:, :LANES] * dq, q[:, LANES:] * dq], axis=-1)
        k_dec = jnp.concatenate([k[:, :LANES] * dk, k[:, LANES:] * dk], axis=-1)
        o_ref[:, h * DV:(h + 1) * DV] = (jnp.dot(scores.astype(BF), v, preferred_element_type=F32)
                                         + _bdot(q_dec, s_old))
        s_ref[h] = gl_ref[h] * s_old + lax.dot_general(
            k_dec.astype(BF), v, (((0,), (0,)), ((), ())), preferred_element_type=F32)

    assert D // CONV_COLS == N_HEADS
    for n in range(N_HEADS):
        conv_block(n)
        retention_head(n)

    @pl.when(i == pl.num_programs(1) - 1)
    def _():
        store_state = pltpu.make_async_copy(s_ref, s_out_hbm.at[b], ssem)
        store_state.start()
        store_state.wait()
        buf_out_ref[0] = tail_ref[...]

    bg = small["b_gates"]
    x1, _, (rec, bin_row) = _post_tokens(c_ref[...], o_ref[...], _silu(proj(O_GRET, D_V)),
                                         _sigmoid(proj(O_GA, D) + bg[:, :D]),
                                         _sigmoid(proj(O_GB, D) + bg[:, D:]), x_ref[0], p)

    sub = lax.broadcasted_iota(jnp.int32, (LANES, PT), 0)
    onehot = sub == bin_row
    earlier = lax.dot_general(onehot.astype(BF), tri_ref[...], (((1,), (1,)), ((), ())),
                              preferred_element_type=F32)
    rank_row = jnp.sum(jnp.where(onehot, earlier + cnt_ref[...], 0.0), axis=0, keepdims=True)
    cnt_ref[...] += jnp.sum(onehot.astype(F32), axis=1, keepdims=True)

    lane = lax.broadcasted_iota(jnp.int32, (PT, LANES), 1)
    x1e_ref[0, :, 0:D] = x1
    x1e_ref[0, :, D:ROW_W] = jnp.where(lane <= AUX_W_HI, rec, 0.0)
    row8 = lax.broadcasted_iota(jnp.int32, (SUBLANES, PT), 0)
    route_ref[...] = jnp.where(row8 == ROUTE_BIN, bin_row,
                               jnp.where(row8 == ROUTE_RANK, rank_row.astype(jnp.int32), 0))
    cnt_out_ref[...] = cnt_ref[...]


def _prompt_call(x, s0, buf0, tabs, consts, small, big, name):
    b, t, _ = x.shape
    cos, sin, dmat, dq, dk, gl = tabs
    z2 = lambda bi, i: (0, 0)
    z3 = lambda bi, i: (0, 0, 0)
    tok = lambda bi, i: (bi, i, 0)
    whole = lambda a: pl.BlockSpec(a.shape, z2 if a.ndim == 2 else z3)
    any_spec = pl.BlockSpec(memory_space=pl.ANY)
    return pl.pallas_call(
        _prompt_kernel,
        grid=(b, t // PT),
        in_specs=[
            pl.BlockSpec((1, PT, D), tok),
            whole(buf0),
            pl.BlockSpec((PT, LANES), lambda bi, i: (i, 0)),
            pl.BlockSpec((PT, LANES), lambda bi, i: (i, 0)),
            whole(dmat), whole(dq), whole(dk),
            pl.BlockSpec(memory_space=pltpu.SMEM),
        ] + [whole(a) for a in consts] + [whole(a) for a in small] + [any_spec] * (1 + len(big)),
        out_specs=[
            pl.BlockSpec((1, PT, ROW_W), tok),
            pl.BlockSpec((SUBLANES, PT), lambda bi, i: (0, bi * (t // PT) + i)),
            pl.BlockSpec((LANES, 1), z2),
            any_spec,
            pl.BlockSpec((1, HALO, D), lambda bi, i: (bi, 0, 0)),
        ],
        out_shape=[
            jax.ShapeDtypeStruct((b, t, ROW_W), F32),
            jax.ShapeDtypeStruct((SUBLANES, b * t), jnp.int32),
            jax.ShapeDtypeStruct((LANES, 1), F32),
            jax.ShapeDtypeStruct((b, N_HEADS, DK, DV), F32),
            jax.ShapeDtypeStruct((b, HALO, D), F32),
        ],
        scratch_shapes=[pltpu.VMEM(w.shape, BF) for w in big] + [
            pltpu.SemaphoreType.DMA((len(big),)),
            pltpu.SemaphoreType.DMA,
            pltpu.VMEM((2, SUBLANES, CONV_COLS // LANES, HALO + PT, LANES), F32),
            pltpu.VMEM((HALO, D), F32),
            pltpu.VMEM((N_HEADS, DK, DV), F32),
            pltpu.VMEM((PT, D), F32),
            pltpu.VMEM((PT, D_V), F32),
            pltpu.VMEM((PT, D), BF),
            pltpu.VMEM((LANES, 1), F32),
        ],
        compiler_params=pltpu.CompilerParams(
            dimension_semantics=("arbitrary", "arbitrary"), vmem_limit_bytes=VMEM_LIMIT),
        name=name,
    )(x, buf0, cos, sin, dmat, dq, dk, gl, *consts, *small, s0, *big)


SEQ_BLK = 2
MXU_ROWS = 16


def _sample_seq_step(step, proj_ref, s_ref, sc_ref, cos_ref, sin_ref, gam_ref, cw_ref, cb_ref,
                     o_ref, c_ref, s_out_ref, sc_out_ref):
    cos = cos_ref[...]
    sin = sin_ref[...]
    row = lax.broadcasted_iota(jnp.int32, (MXU_ROWS, DK), 0)
    for s in range(SEQ_BLK):
        seq = pl.ds(step * SEQ_BLK + s, 1)
        us = proj_ref[seq, O_GLU_A:O_GLU_A + D] * _sigmoid(proj_ref[seq, O_GLU_B:O_GLU_B + D])
        c_ref[seq, :] = (
            jnp.sum(cw_ref[0:CONV_W - 1, :] * sc_ref[s], axis=0, keepdims=True)
            + cw_ref[CONV_W - 1:CONV_W, :] * us + cb_ref[...])
        sc_out_ref[s, 0:CONV_W - 2, :] = sc_ref[s, 1:CONV_W - 1, :]
        sc_out_ref[s, CONV_W - 2:CONV_W - 1, :] = us
        for h in range(N_HEADS):
            q = _rotary(proj_ref[seq, O_Q + h * DK:O_Q + (h + 1) * DK], cos, sin)
            k = _rotary(proj_ref[seq, O_K + h * DK:O_K + (h + 1) * DK], cos, sin) * (DK ** -0.5)
            v = proj_ref[seq, O_V + h * DV:O_V + (h + 1) * DV]
            gam = gam_ref[h]
            qk = jnp.sum(q * k, axis=-1, keepdims=True)
            s_old = s_ref[s, h]
            o8 = _bdot(jnp.broadcast_to(q * gam, (MXU_ROWS, DK)), s_old)
            o_ref[seq, h * DV:(h + 1) * DV] = qk * v + o8[0:1, :]
            k8 = jnp.where(row == 0, jnp.broadcast_to(k, (MXU_ROWS, DK)), 0.0).astype(BF)
            v8 = jnp.broadcast_to(v, (MXU_ROWS, DV)).astype(BF)
            kv = lax.dot_general(k8, v8, (((0,), (0,)), ((), ())), preferred_element_type=F32)
            s_out_ref[s, h] = gam * s_old + kv


def _post_kernel(c_ref, o_ref, gret_ref, gates_ref, x_ref, *rest):
    p = dict(zip(_POST_NAMES, rest[:len(_POST_NAMES)]))
    x1_ref, comb_ref = rest[len(_POST_NAMES):]
    bg = p["b_gates"][...]
    x1, comb, _ = _post_tokens(c_ref[...], o_ref[...], _silu(gret_ref[...]),
                               _sigmoid(gates_ref[:, :D] + bg[:, :D]), _sigmoid(gates_ref[:, D:] + bg[:, D:]),
                               x_ref[...], p)
    x1_ref[...] = x1
    comb_ref[...] = comb


def _post_call(c, o, proj_s, x_s, post_args):
    n = x_s.shape[0]
    z2 = lambda i: (0, 0)
    return pl.pallas_call(
        _post_kernel,
        grid=(1,),
        in_specs=[
            pl.BlockSpec((n, D), z2),
            pl.BlockSpec((n, D_V), z2),
            pl.BlockSpec((n, D_V), lambda i: (0, O_GRET // D_V)),
            pl.BlockSpec((n, 2 * D), lambda i: (0, O_GA // (2 * D))),
            pl.BlockSpec((n, D), z2),
        ] + _post_specs(),
        out_specs=[pl.BlockSpec((n, D), z2), pl.BlockSpec((n, LANES), z2)],
        out_shape=[jax.ShapeDtypeStruct((n, D), F32), jax.ShapeDtypeStruct((n, LANES), F32)],
        compiler_params=pltpu.CompilerParams(
            dimension_semantics=("arbitrary",), vmem_limit_bytes=VMEM_LIMIT),
        name="sample_post",
    )(c, o, proj_s, proj_s, x_s, *post_args)


def _moe_kernel(x1_ref, comb_ref, fg_ref, ng_ref, wg_ref, wu_ref, wd_ref, y_ref, t_ref, acc_ref):
    g = pl.program_id(1)

    @pl.when(g == 0)
    def _():
        t_ref[...] = _rmsnorm(x1_ref[...], fg_ref[...]).astype(BF)
        acc_ref[...] = jnp.zeros_like(acc_ref)

    t = t_ref[...]
    comb = comb_ref[...]
    lane = lax.broadcasted_iota(jnp.int32, comb.shape, 1)
    acc = acc_ref[...]
    for k in range(EPG):
        hg = jnp.dot(t, wg_ref[k], preferred_element_type=F32)
        hu = jnp.dot(t, wu_ref[k], preferred_element_type=F32)
        ce = jnp.sum(jnp.where(lane == g * EPG + k, comb, 0.0), axis=-1, keepdims=True)
        act = (hg * _sigmoid(hg)) * hu * ce
        acc = acc + jnp.dot(act.astype(BF), wd_ref[k], preferred_element_type=F32)
    acc_ref[...] = acc

    @pl.when(g == N_GROUPS - 1)
    def _():
        y_ref[...] = _rmsnorm(x1_ref[...] + acc, ng_ref[...])


def _moe_call(x1, comb, ffn_g, final_g, wg, wu, wd, tm):
    n = x1.shape[0]
    z2 = lambda i, e: (0, 0)
    return pl.pallas_call(
        _moe_kernel,
        grid=(n // tm, N_GROUPS),
        in_specs=[
            pl.BlockSpec((tm, D), lambda i, e: (i, 0)),
            pl.BlockSpec((tm, LANES), lambda i, e: (i, 0)),
            pl.BlockSpec((1, D), z2),
            pl.BlockSpec((1, D), z2),
            pl.BlockSpec((EPG, D, D_EXPERT), lambda i, e: (e, 0, 0)),
            pl.BlockSpec((EPG, D, D_EXPERT), lambda i, e: (e, 0, 0)),
            pl.BlockSpec((EPG, D_EXPERT, D), lambda i, e: (e, 0, 0)),
        ],
        out_specs=pl.BlockSpec((tm, D), lambda i, e: (i, 0)),
        out_shape=jax.ShapeDtypeStruct((n, D), F32),
        scratch_shapes=[pltpu.VMEM((tm, D), BF), pltpu.VMEM((tm, D), F32)],
        compiler_params=pltpu.CompilerParams(
            dimension_semantics=("arbitrary", "arbitrary"), vmem_limit_bytes=VMEM_LIMIT),
        name="moe",
    )(x1, comb, ffn_g, final_g, wg, wu, wd)


def _moe_sparse_kernel(n_seq_steps, slot_ref, e_lo_ref, e_hi_ref, nvalid_ref, nused_ref,
                       rows_hbm, fg_ref, ng_ref, wg_lo, wu_lo, wd_lo, wg_hi, wu_hi, wd_hi,
                       dproj_ref, ds_ref, dsc_ref, dcos_ref, dsin_ref, dgam_ref, dcw_ref, dcb_ref,
                       y_hbm, do_ref, dc_ref, ds_out_ref, dsc_out_ref,
                       inv_ref, xbuf, ybuf, gsem, ssem):
    t = pl.program_id(0)
    n_used = nused_ref[0]
    n_tok = slot_ref.shape[0]
    cur = lax.rem(t, 2)

    @pl.when(t < n_seq_steps)
    def _():
        _sample_seq_step(t, dproj_ref, ds_ref, dsc_ref, dcos_ref, dsin_ref, dgam_ref, dcw_ref, dcb_ref,
                         do_ref, dc_ref, ds_out_ref, dsc_out_ref)

    def hbm_row(ref, tok):
        return ref.at[jnp.right_shift(tok, 3), pl.ds(tok & (SUBLANES - 1), 1), :]

    def gather_copy(grp, sub, tok, buf):
        return pltpu.make_async_copy(hbm_row(rows_hbm, tok), xbuf.at[buf, grp, pl.ds(sub, 1), :],
                                     gsem.at[buf])

    def scatter_copy(grp, sub, tok, buf):
        return pltpu.make_async_copy(ybuf.at[buf, grp, pl.ds(sub, 1), :], hbm_row(y_hbm, tok),
                                     ssem.at[buf])

    def start_rows(copy, tile, buf):
        base = tile * MOE_TM
        n_rows = nvalid_ref[tile]
        n_grp = n_rows // SUBLANES

        def group(g, c):
            for u in range(SUBLANES):
                copy(g, u, inv_ref[base + g * SUBLANES + u], buf).start()
            return c
        lax.fori_loop(0, n_grp, group, 0)

        def tail(r, c):
            copy(n_grp, r - n_grp * SUBLANES, inv_ref[base + r], buf).start()
            return c
        lax.fori_loop(n_grp * SUBLANES, n_rows, tail, 0)

    def wait_rows(copy, tile, buf):
        n_rows = nvalid_ref[tile]
        vbuf, hbm, sem = (xbuf, rows_hbm, gsem) if copy is gather_copy else (ybuf, y_hbm, ssem)
        k = MOE_TM
        while k >= 1:
            @pl.when((n_rows & k) != 0)
            def _(k=k):
                if k >= SUBLANES:
                    vmem = vbuf.at[buf, pl.ds(0, k // SUBLANES)]
                    ext = hbm.at[pl.ds(0, k // SUBLANES)]
                else:
                    vmem = vbuf.at[buf, 0, pl.ds(0, k), :]
                    ext = hbm.at[0, pl.ds(0, k), :]
                pair = (ext, vmem) if copy is gather_copy else (vmem, ext)
                pltpu.make_async_copy(pair[0], pair[1], sem.at[buf]).wait()
            k //= 2

    @pl.when(t == 0)
    def _():
        def fill(tok, c):
            inv_ref[slot_ref[tok]] = tok
            return c
        lax.fori_loop(0, n_tok, fill, 0, unroll=FILL_UNROLL)
        xbuf[...] = jnp.zeros_like(xbuf)
        start_rows(gather_copy, 0, 0)

    @pl.when(t < n_used)
    def _():
        wait_rows(gather_copy, t, cur)

        @pl.when(t + 1 < n_used)
        def _():
            start_rows(gather_copy, t + 1, 1 - cur)

        xt = xbuf[cur].reshape(MOE_TM, ROW_W)
        x1 = xt[:, 0:D]
        aux = xt[:, D:ROW_W]
        tb = _rmsnorm(x1, fg_ref[...]).astype(BF)
        acc = None
        for wg, wu, wd, lane_w in ((wg_lo, wu_lo, wd_lo, AUX_W_LO), (wg_hi, wu_hi, wd_hi, AUX_W_HI)):
            hg = jnp.dot(tb, wg[0], preferred_element_type=F32)
            hu = jnp.dot(tb, wu[0], preferred_element_type=F32)
            act = (hg * _sigmoid(hg)) * hu * aux[:, lane_w:lane_w + 1]
            part = jnp.dot(act.astype(BF), wd[0], preferred_element_type=F32)
            acc = part if acc is None else acc + part
        y = _rmsnorm(x1 + acc, ng_ref[...])

        @pl.when(t >= 2)
        def _():
            wait_rows(scatter_copy, jnp.maximum(t - 2, 0), cur)
        ybuf[cur] = y.reshape(MOE_TM // SUBLANES, SUBLANES, D)
        start_rows(scatter_copy, t, cur)

        @pl.when(t == n_used - 1)
        def _():
            wait_rows(scatter_copy, t, cur)

            @pl.when(t >= 1)
            def _():
                wait_rows(scatter_copy, jnp.maximum(t - 1, 0), 1 - cur)


def _bin_tables():
    lo, hi = [], []
    for g in range(N_GROUPS):
        for a in range(EPG):
            for b in range(a + 1, EPG):
                lo.append(g * EPG + a)
                hi.append(g * EPG + b)
    return jnp.array(lo, jnp.int32), jnp.array(hi, jnp.int32)


def _moe_sparse_call(rows, route, counts, ffn_g, final_g, wg, wu, wd, decode):
    proj_s, state_ret, state_conv, cos_s, sin_s, gam, cw, cb = decode
    n_seq = proj_s.shape[0]
    n_seq_steps = n_seq // SEQ_BLK
    n = route.shape[1]
    max_tiles = n // MOE_TM + N_BINS
    assert max_tiles >= n_seq_steps
    padded = ((counts + MOE_TM - 1) // MOE_TM) * MOE_TM
    ends = jnp.cumsum(padded)
    offs = ends - padded
    n_used = ends[-1] // MOE_TM
    slot = offs[route[ROUTE_BIN]] + route[ROUTE_RANK]
    tile_start = jnp.minimum(jnp.arange(max_tiles, dtype=jnp.int32), n_used - 1) * MOE_TM
    tile_bin = jnp.sum((tile_start[:, None] >= ends[None, :]).astype(jnp.int32), axis=1)
    bin_lo, bin_hi = _bin_tables()
    e_lo = bin_lo[tile_bin]
    e_hi = bin_hi[tile_bin]
    tile_id = jnp.arange(max_tiles, dtype=jnp.int32)
    n_valid = jnp.where(tile_id < n_used,
                        jnp.clip(counts[tile_bin] - (tile_start - offs[tile_bin]), 0, MOE_TM), 0)

    z2 = lambda t, *_: (0, 0)
    w_lo_map = lambda t, slot_r, lo_r, hi_r, nv_r, nu_r: (lo_r[t], 0, 0)
    w_hi_map = lambda t, slot_r, lo_r, hi_r, nv_r, nu_r: (hi_r[t], 0, 0)
    seq3 = lambda t, *_: (jnp.minimum(t, n_seq_steps - 1), 0, 0)
    seq4 = lambda t, *_: (jnp.minimum(t, n_seq_steps - 1), 0, 0, 0)
    y, o, c, s_new, sc_new = pl.pallas_call(
        functools.partial(_moe_sparse_kernel, n_seq_steps),
        grid_spec=pltpu.PrefetchScalarGridSpec(
            num_scalar_prefetch=5,
            grid=(max_tiles,),
            in_specs=[
                pl.BlockSpec(memory_space=pl.ANY),
                pl.BlockSpec((1, D), z2),
                pl.BlockSpec((1, D), z2),
                pl.BlockSpec((1, D, D_EXPERT), w_lo_map),
                pl.BlockSpec((1, D, D_EXPERT), w_lo_map),
                pl.BlockSpec((1, D_EXPERT, D), w_lo_map),
                pl.BlockSpec((1, D, D_EXPERT), w_hi_map),
                pl.BlockSpec((1, D, D_EXPERT), w_hi_map),
                pl.BlockSpec((1, D_EXPERT, D), w_hi_map),
                pl.BlockSpec((n_seq, D_IN), z2),
                pl.BlockSpec((SEQ_BLK, N_HEADS, DK, DV), seq4),
                pl.BlockSpec((SEQ_BLK, CONV_W - 1, D), seq3),
                pl.BlockSpec((1, LANES), z2),
                pl.BlockSpec((1, LANES), z2),
                pl.BlockSpec(memory_space=pltpu.SMEM),
                pl.BlockSpec((CONV_W, D), z2),
                pl.BlockSpec((1, D), z2),
            ],
            out_specs=[
                pl.BlockSpec(memory_space=pl.ANY),
                pl.BlockSpec((n_seq, D_V), z2),
                pl.BlockSpec((n_seq, D), z2),
                pl.BlockSpec((SEQ_BLK, N_HEADS, DK, DV), seq4),
                pl.BlockSpec((SEQ_BLK, CONV_W - 1, D), seq3),
            ],
            scratch_shapes=[
                pltpu.SMEM((max_tiles * MOE_TM,), jnp.int32),
                pltpu.VMEM((2, MOE_TM // SUBLANES, SUBLANES, ROW_W), F32),
                pltpu.VMEM((2, MOE_TM // SUBLANES, SUBLANES, D), F32),
                pltpu.SemaphoreType.DMA((2,)),
                pltpu.SemaphoreType.DMA((2,)),
            ],
        ),
        out_shape=[
            jax.ShapeDtypeStruct((n // SUBLANES, SUBLANES, D), F32),
            jax.ShapeDtypeStruct((n_seq, D_V), F32),
            jax.ShapeDtypeStruct((n_seq, D), F32),
            jax.ShapeDtypeStruct(state_ret.shape, F32),
            jax.ShapeDtypeStruct(state_conv.shape, F32),
        ],
        compiler_params=pltpu.CompilerParams(
            dimension_semantics=("arbitrary",), vmem_limit_bytes=VMEM_LIMIT),
        name="moe_sparse",
    )(slot, e_lo, e_hi, n_valid, n_used.reshape(1).astype(jnp.int32),
      rows.reshape(n // SUBLANES, SUBLANES, ROW_W), ffn_g, final_g, wg, wu, wd, wg, wu, wd,
      proj_s, state_ret, state_conv, cos_s, sin_s, gam, cw, cb)
    return y.reshape(n, D), o, c, s_new, sc_new


def _rope_tables(pos):
    half = DK // 2
    freqs = np.power(np.float64(ROPE_BASE), -np.arange(half, dtype=np.float64) / half)
    ang = pos.astype(np.float64)[:, None] * freqs[None, :]
    return jnp.asarray(np.cos(ang), F32), jnp.asarray(np.sin(ang), F32)


def _decay_tables(L):
    lg = np.log(1.0 - np.exp2(-5.0 - np.arange(N_HEADS, dtype=np.float64)))
    idx = np.arange(L, dtype=np.float64)
    diff = idx[:, None] - idx[None, :]
    dmat = np.where((diff >= 0)[None], np.exp(np.maximum(diff, 0.0)[None] * lg[:, None, None]), 0.0)
    dq = np.exp((idx[:, None] + 1.0) * lg[None, :]).T
    dk = np.exp((L - 1.0 - idx)[:, None] * lg[None, :]).T
    gl = np.exp(L * lg)
    bl = lambda a: jnp.asarray(np.broadcast_to(a[:, :, None], (N_HEADS, L, LANES)), F32)
    return jnp.asarray(dmat, F32), bl(dq), bl(dk), jnp.asarray(gl, F32)


def kernel(x_prompt, x_sample, state_ret, state_conv, meta_tokens, norm_mix_g, w_in, b_gates, conv_w, conv_b, conv_ln_g, conv_ln_b, w_conv_out, ret_gn_g, w_ret_out, w_o, norm_ffn_g, w_coarse, b_coarse, w_fine, b_fine, w_gate_e, w_up_e, w_down_e, norm_final_g):
    bp, seq, _ = x_prompt.shape
    ns = x_sample.shape[0]
    assert w_in.shape[0] == 1, "one layer"
    assert x_prompt.shape[2] == D and seq % PT == 0 and (bp * seq) % MOE_TM == 0
    assert x_sample.shape[1:] == (1, D) and ns % SEQ_BLK == 0 and ns % SUBLANES == 0
    assert meta_tokens.shape == (N_META, D) and N_META <= PT
    l = 0
    row = lambda a: a.reshape(1, -1)

    w_in_bf = w_in[l].astype(BF)
    wr = jnp.zeros((D, LANES), F32).at[:, :N_EXPERTS].set(w_fine[l]).at[:, N_EXPERTS:N_EXPERTS + N_GROUPS].set(w_coarse[l])
    wr_hi = wr.astype(BF)
    wr_lo = (wr - wr_hi.astype(F32)).astype(BF)
    br = jnp.zeros((1, LANES), F32).at[0, :N_EXPERTS].set(b_fine[l]).at[0, N_EXPERTS:N_EXPERTS + N_GROUPS].set(b_coarse[l])
    post_args = (row(conv_ln_g[l]), row(conv_ln_b[l]), w_conv_out[l].astype(BF), row(ret_gn_g[l]),
                 w_ret_out[l].astype(BF), row(b_gates[l]), w_o[l].astype(BF), row(norm_ffn_g[l]),
                 wr_hi, wr_lo, br)
    wg = w_gate_e[l].astype(BF)
    wu = w_up_e[l].astype(BF)
    wd = w_down_e[l].astype(BF)
    mix_g = row(norm_mix_g[l])
    cw = conv_w[l]
    cb = row(conv_b[l])

    small = dict(zip(_POST_NAMES, post_args))
    small_args = tuple(small[k] for k in _SMALL_NAMES)
    big_args = (w_in_bf,) + tuple(small[k] for k in _BIG_NAMES[1:])
    r = np.arange(PT)
    tri = jnp.asarray(r[None, :] < r[:, None], BF)
    consts = (cw, cb, tri, mix_g)
    decay = _decay_tables(PT)
    x_meta = jnp.concatenate([jnp.zeros((PT - N_META, D), F32), meta_tokens])[None]
    pos_meta = np.maximum(np.arange(PT) - (PT - N_META), 0)
    _, _, _, s_meta, buf_meta = _prompt_call(
        x_meta, jnp.zeros((N_HEADS, DK, DV), F32), jnp.zeros((HALO, D), F32),
        _rope_tables(pos_meta) + decay, consts, small_args, big_args, "meta_state")
    pos_p = N_META + np.arange(seq)
    rows_p, route_p, counts, s_new_p, buf_p = _prompt_call(
        x_prompt, s_meta[0], buf_meta[0], _rope_tables(pos_p) + decay, consts, small_args, big_args,
        "prompt_mix")

    xs = x_sample.reshape(ns, D)
    proj_s = _proj_call(xs, mix_g, w_in_bf, tm=ns)
    pos_s = np.full((1,), PAST_LEN)
    cos_s, sin_s = _rope_tables(pos_s)
    _, _, _, gam = _decay_tables(1)

    y_p, o_s, c_s, s_new_s, buf_s = _moe_sparse_call(
        rows_p.reshape(bp * seq, ROW_W), route_p,
        counts[:N_BINS, 0].astype(jnp.int32), row(norm_ffn_g[l]), row(norm_final_g), wg, wu, wd,
        (proj_s, state_ret[l], state_conv[l], cos_s, sin_s, gam, cw, cb))

    x1_s, comb_s = _post_call(c_s, o_s, proj_s, xs, post_args)
    y_s = _moe_call(x1_s, comb_s, row(norm_ffn_g[l]), row(norm_final_g), wg, wu, wd, tm=ns).reshape(ns, 1, D)

    return (y_p.reshape(bp, seq, D), y_s, s_new_p[None], buf_p[:, HALO - (CONV_W - 1):][None],
            s_new_s[None], buf_s[None])
```
